```python
import math
import jax, jax.numpy as jnp
from jax import lax
import numpy as np

D_MODEL = 1024
BATCH = 4
SEQ = 8192
DEPTH = 2
DEC_BATCH = 2
DEC_SEQ = 8192
PAST_LEN = 128

GRID_W = 64
PLE_DIM = 256
EPS = 1e-6
SSD_HEADS = 32
SSD_HEADDIM = 64
D_INNER = SSD_HEADS * SSD_HEADDIM
SSD_GROUPS = 4
SSD_STATE = 128
SSD_CHUNK = 128
CONV_W = 5
GN = SSD_GROUPS * SSD_STATE
XBC_DIM = D_INNER + 2 * GN
HEAD_DIM = 128
GQA_Q_HEADS = 16
GQA_KV_HEADS = 4
GQA_REP = GQA_Q_HEADS // GQA_KV_HEADS
GQA_WIDTH = GQA_Q_HEADS * HEAD_DIM
GQA_KV_WIDTH = GQA_KV_HEADS * HEAD_DIM
Q_BLOCK = 128
ROPE_THETA = 10000.0
ATTN_SCALE = HEAD_DIM ** -0.5
DIL_PATTERNS = ((128, 1), (512, 4), (2048, 16))
DIL_HEADS_PER_GROUP = 4
DIL_HEADS = len(DIL_PATTERNS) * DIL_HEADS_PER_GROUP
DIL_WIDTH = DIL_HEADS * HEAD_DIM
DIL_OUT = DIL_HEADS_PER_GROUP * HEAD_DIM
N_BUCKETS = 32
REL_MAX_DIST = 2048
FFN_DIM = ((8 * D_MODEL // 3 + 255) // 256) * 256
N_BRANCHES = 3
IN_WIDTHS = (D_INNER, XBC_DIM, 2 * SSD_HEADS, GQA_WIDTH, GQA_KV_WIDTH, GQA_KV_WIDTH,
             DIL_WIDTH, DIL_WIDTH, DIL_WIDTH, N_BRANCHES * D_MODEL)
IN_TOTAL = sum(IN_WIDTHS)

kernel_name = "hybrid_gated_ssd_gqa_dilated_encoder"


def rmsnorm(x, g):
    xf = x.astype(jnp.float32)
    y = xf * lax.rsqrt(jnp.mean(xf * xf, axis=-1, keepdims=True) + EPS)
    return (y * g.astype(jnp.float32)).astype(x.dtype)


def depthwise_conv(x, w):
    pad = CONV_W // 2
    return lax.conv_general_dilated(
        x, w[:, None, :].astype(x.dtype), window_strides=(1,), padding=[(pad, pad)],
        dimension_numbers=('NWC', 'WIO', 'NWC'), feature_group_count=x.shape[-1])


def ssd_scan(xs, dt, a, bm, cm):
    b, s = xs.shape[:2]
    nc = s // SSD_CHUNK
    hg = SSD_HEADS // SSD_GROUPS
    shp = (b, nc, SSD_CHUNK, SSD_GROUPS)
    xc = (xs * dt[..., None]).reshape(*shp, hg, SSD_HEADDIM)
    acs = jnp.cumsum((dt * a).reshape(*shp, hg), axis=2)
    bc = bm.reshape(*shp, SSD_STATE)
    cc = cm.reshape(*shp, SSD_STATE)
    causal = jnp.tril(jnp.ones((SSD_CHUNK, SSD_CHUNK), dtype=bool))
    seg = acs[:, :, :, None] - acs[:, :, None, :]
    decay = jnp.exp(jnp.where(causal[:, :, None, None], seg, -jnp.inf))
    cb = jnp.einsum('bclgn,bcsgn->bclsg', cc, bc)
    y_diag = jnp.einsum('bclsg,bclsgh,bcsghp->bclghp', cb, decay, xc)
    decay_end = jnp.exp(acs[:, :, -1:] - acs)
    states = jnp.einsum('bclgn,bclgh,bclghp->bcghpn', bc, decay_end, xc)
    chunk_decay = jnp.exp(acs[:, :, -1])

    def step(h_prev, inp):
        st, dec = inp
        return h_prev * dec[..., None, None] + st, h_prev

    init = jnp.zeros_like(states[:, 0])
    _, h_in = lax.scan(step, init, (jnp.moveaxis(states, 1, 0), jnp.moveaxis(chunk_decay, 1, 0)))
    h_in = jnp.moveaxis(h_in, 0, 1)
    y_off = jnp.einsum('bclgn,bcghpn,bclgh->bclghp', cc, h_in, jnp.exp(acs))
    return (y_diag + y_off).reshape(b, s, SSD_HEADS, SSD_HEADDIM)


def flip_seq(t):
    return jnp.flip(t, axis=1)


def ssd_branch(z, xbc, dt_raw, conv_w, conv_b, dt_bias, a_log, d_skip, g_norm):
    b, s, _ = z.shape
    xbc = jax.nn.silu(depthwise_conv(xbc, conv_w) + conv_b).astype(jnp.float32)
    xs = xbc[..., :D_INNER].reshape(b, s, SSD_HEADS, SSD_HEADDIM)
    bm = xbc[..., D_INNER:D_INNER + GN].reshape(b, s, SSD_GROUPS, SSD_STATE)
    cm = xbc[..., D_INNER + GN:].reshape(b, s, SSD_GROUPS, SSD_STATE)
    dt = jax.nn.softplus(dt_raw.astype(jnp.float32).reshape(b, s, 2, SSD_HEADS)
                         + dt_bias.astype(jnp.float32))
    a = -jnp.exp(a_log.astype(jnp.float32))
    y_fwd = ssd_scan(xs, dt[:, :, 0], a[0], bm, cm)
    y_bwd = flip_seq(ssd_scan(flip_seq(xs), flip_seq(dt[:, :, 1]), a[1], flip_seq(bm), flip_seq(cm)))
    y = y_fwd + y_bwd + xs * d_skip.astype(jnp.float32)[:, None]
    y = y.reshape(b, s, D_INNER) * jax.nn.silu(z.astype(jnp.float32))
    return rmsnorm(y, g_norm).astype(z.dtype)


def axial_rope(rows):
    row = jnp.repeat(jnp.arange(rows), GRID_W).astype(jnp.float32)
    col = jnp.tile(jnp.arange(GRID_W), rows).astype(jnp.float32)
    n_pairs = HEAD_DIM // 4
    inv = ROPE_THETA ** (-jnp.arange(n_pairs, dtype=jnp.float32) / n_pairs)
    ang = jnp.concatenate([row[:, None] * inv, col[:, None] * inv], axis=-1)
    return jnp.cos(ang), jnp.sin(ang)


def apply_rope(x, cos, sin):
    xf = x.astype(jnp.float32).reshape(*x.shape[:-1], HEAD_DIM // 2, 2)
    x0, x1 = xf[..., 0], xf[..., 1]
    c = cos[None, :, None, :]
    s = sin[None, :, None, :]
    out = jnp.stack([x0 * c - x1 * s, x0 * s + x1 * c], axis=-1).reshape(x.shape)
    return out.astype(x.dtype)


def gqa_attention(q, k, v):
    b, s = q.shape[:2]
    nb = s // Q_BLOCK
    qb = q.reshape(b, nb, Q_BLOCK, GQA_KV_HEADS, GQA_REP, HEAD_DIM).transpose(1, 0, 2, 3, 4, 5)

    def block(qblk):
        logits = jnp.einsum('bqkrd,bskd->bkrqs', qblk, k).astype(jnp.float32) * ATTN_SCALE
        probs = jax.nn.softmax(logits, axis=-1).astype(v.dtype)
        return jnp.einsum('bkrqs,bskd->bqkrd', probs, v)

    out = lax.map(block, qb)
    return out.transpose(1, 0, 2, 3, 4, 5).reshape(b, s, GQA_WIDTH)


def t5_bucket(rel):
    nb = N_BUCKETS // 2
    max_exact = nb // 2
    ret = jnp.where(rel > 0, nb, 0)
    n = jnp.abs(rel)
    nf = jnp.maximum(n, 1).astype(jnp.float32)
    large = max_exact + (jnp.log(nf / max_exact) / math.log(REL_MAX_DIST / max_exact)
                         * (nb - max_exact)).astype(jnp.int32)
    large = jnp.minimum(large, nb - 1)
    return ret + jnp.where(n < max_exact, n, large)


def dilated_biases(rel_bias):
    out = []
    for g, (window, dil) in enumerate(DIL_PATTERNS):
        half = window // (2 * dil)
        dist = jnp.arange(-half, half + 1, dtype=jnp.int32) * dil
        tbl = rel_bias[t5_bucket(dist)]
        out.append(tbl[:, g * DIL_HEADS_PER_GROUP:(g + 1) * DIL_HEADS_PER_GROUP].T)
    return out


def to_residue(t, d):
    b, s = t.shape[:2]
    rest = t.shape[2:]
    return t.reshape(b, s // d, d, *rest).swapaxes(1, 2).reshape(b * d, s // d, *rest)


def from_residue(t, d, b):
    l = t.shape[1]
    rest = t.shape[2:]
    return t.reshape(b, d, l, *rest).swapaxes(1, 2).reshape(b, l * d, *rest)


def banded_attention(q, k, v, bias, half):
    n, l, h, dh = q.shape
    blk = half
    nb = -(-l // blk)
    lp = nb * blk
    qp = jnp.pad(q, ((0, 0), (0, lp - l), (0, 0), (0, 0))).reshape(n, nb, blk, h, dh)

    def windows(t):
        tp = jnp.pad(t, ((0, 0), (blk, lp - l + blk), (0, 0), (0, 0))).reshape(n, nb + 2, blk, h, dh)
        return jnp.concatenate([tp[:, :-2], tp[:, 1:-1], tp[:, 2:]], axis=2)

    kw, vw = windows(k), windows(v)
    qpos = jnp.arange(lp).reshape(nb, blk)
    kpos = jnp.arange(nb)[:, None] * blk - blk + jnp.arange(3 * blk)[None, :]
    rel = kpos[:, None, :] - qpos[:, :, None]
    inside = (kpos >= 0) & (kpos < l)
    valid = ((jnp.abs(rel) <= half) & inside[:, None, :]) | (rel == 0)
    bias_blk = jnp.take(bias.astype(jnp.float32), jnp.clip(rel + half, 0, 2 * half), axis=1)
    logits = (jnp.einsum('nbqhd,nbkhd->nbhqk', qp, kw).astype(jnp.float32) * ATTN_SCALE
              + jnp.moveaxis(bias_blk, 0, 1)[None])
    logits = jnp.where(valid[None, :, None], logits, -jnp.inf)
    lse = jax.nn.logsumexp(logits, axis=-1)
    probs = jnp.exp(logits - lse[..., None]).astype(v.dtype)
    out = jnp.einsum('nbhqk,nbkhd->nbqhd', probs, vw).reshape(n, lp, h, dh)[:, :l]
    lse = jnp.moveaxis(lse, 2, 3).reshape(n, lp, h)[:, :l]
    return out, lse


def dilated_attention(q, k, v, biases):
    b, s = q.shape[:2]
    outs, lses = [], []
    for g, (window, dil) in enumerate(DIL_PATTERNS):
        hs = slice(g * DIL_HEADS_PER_GROUP, (g + 1) * DIL_HEADS_PER_GROUP)
        o, lse = banded_attention(to_residue(q[:, :, hs], dil), to_residue(k[:, :, hs], dil),
                                  to_residue(v[:, :, hs], dil), biases[g], window // (2 * dil))
        outs.append(from_residue(o, dil, b))
        lses.append(from_residue(lse, dil, b))
    wts = jax.nn.softmax(jnp.stack(lses), axis=0)
    out = jnp.sum(wts[..., None] * jnp.stack(outs).astype(jnp.float32), axis=0)
    return out.reshape(b, s, DIL_OUT).astype(q.dtype)


def encoder_layer(x, p_i, i, P, cos, sin, dil_bias):
    b, s, _ = x.shape
    h = rmsnorm(x, P['g_pre_mix'][i])
    proj = h @ P['w_in'][i]
    split_at = [int(c) for c in np.cumsum(IN_WIDTHS)[:-1]]
    z, xbc, dt_raw, gq, gk, gv, dq, dk, dv, gates = jnp.split(proj, split_at, axis=-1)
    y_ssd = ssd_branch(z, xbc, dt_raw, P['conv_w'][i], P['conv_b'][i], P['dt_bias'][i],
                       P['a_log'][i], P['d_skip'][i], P['g_ssd'][i])
    q = apply_rope(rmsnorm(gq.reshape(b, s, GQA_Q_HEADS, HEAD_DIM), P['g_q'][i]), cos, sin)
    k = apply_rope(rmsnorm(gk.reshape(b, s, GQA_KV_HEADS, HEAD_DIM), P['g_k'][i]), cos, sin)
    y_gqa = gqa_attention(q, k, gv.reshape(b, s, GQA_KV_HEADS, HEAD_DIM))
    y_dil = dilated_attention(dq.reshape(b, s, DIL_HEADS, HEAD_DIM), dk.reshape(b, s, DIL_HEADS, HEAD_DIM),
                              dv.reshape(b, s, DIL_HEADS, HEAD_DIM), dil_bias)
    br = jnp.stack([y_ssd @ P['w_br_ssd'][i], y_gqa @ P['w_br_gqa'][i], y_dil @ P['w_br_dil'][i]], axis=2)
    g = jax.nn.sigmoid(gates.reshape(b, s, N_BRANCHES, D_MODEL))
    mix = jnp.sum(g * br, axis=2) @ P['w_out'][i]
    x = x + rmsnorm(mix, P['g_post_mix'][i])
    h = rmsnorm(x, P['g_pre_ffn'][i])
    ff = (jax.nn.silu(h @ P['w_gate'][i]) * (h @ P['w_up'][i])) @ P['w_down'][i]
    x = x + rmsnorm(ff, P['g_post_ffn'][i])
    ple_gate = jax.nn.sigmoid(rmsnorm(x, P['g_ple'][i]) @ P['w_ple_gate'][i])
    return x + (p_i @ P['w_ple'][i]) * ple_gate


def trunk(x, p, P, rel_bias):
    rows = x.shape[1] // GRID_W
    cos, sin = axial_rope(rows)
    dil_bias = dilated_biases(rel_bias)
    for i in range(DEPTH):
        x = encoder_layer(x, p[i], i, P, cos, sin, dil_bias)
    return x


def setup_inputs(seed: int = 0) -> dict:
    key = jax.random.key(seed)
    ks = jax.random.split(key, 28)
    f32 = jnp.float32

    def nrm(k, shape, fan_in):
        return jax.random.normal(k, shape, f32) * fan_in ** -0.5

    def gain(k, shape):
        return 1.0 + 0.02 * jax.random.normal(k, shape, f32)

    dt = jnp.exp(jax.random.uniform(ks[7], (DEPTH, 2, SSD_HEADS), f32, math.log(1e-3), math.log(1e-1)))
    return {
        'x_prompt': jax.random.normal(ks[0], (BATCH, SEQ, D_MODEL), f32),
        'x_sample': jax.random.normal(ks[1], (DEC_BATCH, DEC_SEQ, D_MODEL), f32),
        'p_prompt': jax.random.normal(ks[2], (DEPTH, BATCH, SEQ, PLE_DIM), f32),
        'p_sample': jax.random.normal(ks[3], (DEPTH, DEC_BATCH, DEC_SEQ, PLE_DIM), f32),
        'w_in': nrm(ks[4], (DEPTH, D_MODEL, IN_TOTAL), D_MODEL),
        'conv_w': nrm(ks[5], (DEPTH, CONV_W, XBC_DIM), CONV_W),
        'conv_b': 0.01 * jax.random.normal(ks[6], (DEPTH, XBC_DIM), f32),
        'dt_bias': dt + jnp.log(-jnp.expm1(-dt)),
        'a_log': jnp.log(jax.random.uniform(ks[8], (DEPTH, 2, SSD_HEADS), f32, 1.0, 16.0)),
        'd_skip': gain(ks[9], (DEPTH, SSD_HEADS)),
        'g_ssd': gain(ks[10], (DEPTH, D_INNER)),
        'g_q': gain(ks[11], (DEPTH, HEAD_DIM)),
        'g_k': gain(ks[12], (DEPTH, HEAD_DIM)),
        'w_br_ssd': nrm(ks[13], (DEPTH, D_INNER, D_MODEL), D_INNER),
        'w_br_gqa': nrm(ks[14], (DEPTH, GQA_WIDTH, D_MODEL), GQA_WIDTH),
        'w_br_dil': nrm(ks[15], (DEPTH, DIL_OUT, D_MODEL), DIL_OUT),
        'w_out': nrm(ks[16], (DEPTH, D_MODEL, D_MODEL), D_MODEL),
        'g_pre_mix': gain(ks[17], (DEPTH, D_MODEL)),
        'g_post_mix': gain(ks[18], (DEPTH, D_MODEL)),
        'g_pre_ffn': gain(ks[19], (DEPTH, D_MODEL)),
        'g_post_ffn': gain(ks[20], (DEPTH, D_MODEL)),
        'w_gate': nrm(ks[21], (DEPTH, D_MODEL, FFN_DIM), D_MODEL),
        'w_up': nrm(ks[22], (DEPTH, D_MODEL, FFN_DIM), D_MODEL),
        'w_down': nrm(ks[23], (DEPTH, FFN_DIM, D_MODEL), FFN_DIM),
        'w_ple': nrm(ks[24], (DEPTH, PLE_DIM, D_MODEL), PLE_DIM),
        'g_ple': gain(ks[25], (DEPTH, D_MODEL)),
        'w_ple_gate': nrm(ks[26], (DEPTH, D_MODEL, D_MODEL), D_MODEL),
        'rel_bias': 0.1 * jax.random.normal(ks[27], (N_BUCKETS, DIL_HEADS), f32),
    }


def reference(x_prompt, x_sample, p_prompt, p_sample, w_in, conv_w, conv_b, dt_bias, a_log, d_skip,
              g_ssd, g_q, g_k, w_br_ssd, w_br_gqa, w_br_dil, w_out, g_pre_mix, g_post_mix,
              g_pre_ffn, g_post_ffn, w_gate, w_up, w_down, w_ple, g_ple, w_ple_gate, rel_bias):
    P = dict(w_in=w_in, conv_w=conv_w, conv_b=conv_b, dt_bias=dt_bias, a_log=a_log, d_skip=d_skip,
             g_ssd=g_ssd, g_q=g_q, g_k=g_k, w_br_ssd=w_br_ssd, w_br_gqa=w_br_gqa, w_br_dil=w_br_dil,
             w_out=w_out, g_pre_mix=g_pre_mix, g_post_mix=g_post_mix, g_pre_ffn=g_pre_ffn,
             g_post_ffn=g_post_ffn, w_gate=w_gate, w_up=w_up, w_down=w_down, w_ple=w_ple,
             g_ple=g_ple, w_ple_gate=w_ple_gate)
    y_prompt = trunk(x_prompt, p_prompt, P, rel_bias)
    y_sample = trunk(x_sample, p_sample, P, rel_bias)
    return (y_prompt, y_sample)
```

```python
import functools
import math

import jax
import jax.numpy as jnp
import numpy as np
from jax import lax
from jax.experimental import pallas as pl
from jax.experimental.pallas import tpu as pltpu

F32 = jnp.float32
BF16 = jnp.bfloat16

D_MODEL = 1024
GRID_W = 64
PLE_DIM = 256
EPS = 1e-6
SSD_HEADS = 32
SSD_HEADDIM = 64
D_INNER = SSD_HEADS * SSD_HEADDIM
SSD_GROUPS = 4
SSD_STATE = 128
SSD_CHUNK = 128
CONV_W = 5
GN = SSD_GROUPS * SSD_STATE
XBC_DIM = D_INNER + 2 * GN
HEAD_DIM = 128
GQA_Q_HEADS = 16
GQA_KV_HEADS = 4
GQA_REP = GQA_Q_HEADS // GQA_KV_HEADS
GQA_WIDTH = GQA_Q_HEADS * HEAD_DIM
GQA_KV_WIDTH = GQA_KV_HEADS * HEAD_DIM
ROPE_THETA = 10000.0
ATTN_SCALE = HEAD_DIM ** -0.5
DIL_PATTERNS = ((128, 1), (512, 4), (2048, 16))
DIL_HEADS_PER_GROUP = 4
DIL_HEADS = len(DIL_PATTERNS) * DIL_HEADS_PER_GROUP
DIL_WIDTH = DIL_HEADS * HEAD_DIM
DIL_OUT = DIL_HEADS_PER_GROUP * HEAD_DIM
DIL_HALF = 64
N_BUCKETS = 32
REL_MAX_DIST = 2048
FFN_DIM = ((8 * D_MODEL // 3 + 255) // 256) * 256
N_BRANCHES = 3
IN_WIDTHS = (D_INNER, XBC_DIM, 2 * SSD_HEADS, GQA_WIDTH, GQA_KV_WIDTH, GQA_KV_WIDTH,
             DIL_WIDTH, DIL_WIDTH, DIL_WIDTH, N_BRANCHES * D_MODEL)

LANES = 128
SUBLANES = 8
VMEM_LIMIT = 56 * 1024 * 1024
NEG_BIG = -1e30


def _cparams(*sem):
    return pltpu.CompilerParams(dimension_semantics=sem, vmem_limit_bytes=VMEM_LIMIT)


def _rms(x, g):
    ms = jnp.mean(x * x, axis=-1, keepdims=True)
    return x * lax.rsqrt(ms + EPS) * g


def _sigmoid(x):
    return 1.0 / (1.0 + jnp.exp(-x))


def _silu(x):
    return x * _sigmoid(x)


def _softplus(x):
    return jnp.maximum(x, 0.0) + jnp.log1p(jnp.exp(-jnp.abs(x)))


def _dot(a, b):
    return jnp.dot(a, b, preferred_element_type=F32)


def _dot_nt(a, b):
    return lax.dot_general(a, b, (((1,), (1,)), ((), ())), preferred_element_type=F32)


def _split3(x):
    hi = x.astype(BF16)
    r1 = x - hi.astype(F32)
    mid = r1.astype(BF16)
    lo = (r1 - mid.astype(F32)).astype(BF16)
    return hi, mid, lo


def _dot_exact_lhs(x, m_bf16, pieces=3):
    parts = _split3(x)[:pieces]
    acc = _dot(parts[0], m_bf16)
    for p in parts[1:]:
        acc = acc + _dot(p, m_bf16)
    return acc


def _dot_exact_rhs(m_bf16, x, pieces=3):
    parts = _split3(x)[:pieces]
    acc = _dot(m_bf16, parts[0])
    for p in parts[1:]:
        acc = acc + _dot(m_bf16, p)
    return acc


def _prenorm_kernel(x_ref, g_ref, o_ref):
    o_ref[...] = _rms(x_ref[...], g_ref[...]).astype(o_ref.dtype)


def _prenorm(x2d, g, bm=1024):
    t, d = x2d.shape
    return pl.pallas_call(
        _prenorm_kernel,
        grid=(t // bm,),
        in_specs=[pl.BlockSpec((bm, d), lambda i: (i, 0)),
                  pl.BlockSpec((1, d), lambda i: (0, 0))],
        out_specs=pl.BlockSpec((bm, d), lambda i: (i, 0)),
        out_shape=jax.ShapeDtypeStruct((t, d), BF16),
        compiler_params=_cparams("parallel"),
        name="prenorm",
    )(x2d, g.reshape(1, d))


def _proj_kernel(h_ref, w_ref, o_ref, *, act, nc):
    h = h_ref[...]
    n = o_ref.shape[-1]
    for c in range(0, n, nc):
        y = _dot(h, w_ref[:, c:c + nc])
        if act == "silu":
            y = _silu(y)
        elif act == "sigmoid":
            y = _sigmoid(y)
        o_ref[:, c:c + nc] = y.astype(o_ref.dtype)


def _proj(h, w, out_dtype, act=None, bm=512, name="proj"):
    t, d = h.shape
    n = w.shape[1]
    nc = 512 if n % 512 == 0 else n
    return pl.pallas_call(
        functools.partial(_proj_kernel, act=act, nc=nc),
        grid=(t // bm,),
        in_specs=[pl.BlockSpec((bm, d), lambda i: (i, 0)),
                  pl.BlockSpec((d, n), lambda i: (0, 0))],
        out_specs=pl.BlockSpec((bm, n), lambda i: (i, 0)),
        out_shape=jax.ShapeDtypeStruct((t, n), out_dtype),
        compiler_params=_cparams("parallel"),
        name=name,
    )(h, w)


def _proj_qk_kernel(h_ref, w_ref, g_ref, cos_ref, sin_ref, o_ref, *, scale):
    h = h_ref[...]
    g = g_ref[...]
    cosf = cos_ref[...]
    sinf = sin_ref[...]
    for hd in range(o_ref.shape[-1] // HEAD_DIM):
        sl = slice(hd * HEAD_DIM, (hd + 1) * HEAD_DIM)
        y = _rms(_dot(h, w_ref[:, sl]), g)
        y = y * cosf + pltpu.roll(y, HEAD_DIM // 2, 1) * sinf
        if scale != 1.0:
            y = y * scale
        o_ref[:, sl] = y.astype(o_ref.dtype)


def _proj_qk(h, w, g, cosf, sinf, scale, seq, bm=512, name="proj_qk"):
    t, d = h.shape
    n = w.shape[1]
    nsb = seq // bm
    return pl.pallas_call(
        functools.partial(_proj_qk_kernel, scale=scale),
        grid=(t // bm,),
        in_specs=[pl.BlockSpec((bm, d), lambda i: (i, 0)),
                  pl.BlockSpec((d, n), lambda i: (0, 0)),
                  pl.BlockSpec((1, HEAD_DIM), lambda i: (0, 0)),
                  pl.BlockSpec((bm, HEAD_DIM), lambda i: (i % nsb, 0)),
                  pl.BlockSpec((bm, HEAD_DIM), lambda i: (i % nsb, 0))],
        out_specs=pl.BlockSpec((bm, n), lambda i: (i, 0)),
        out_shape=jax.ShapeDtypeStruct((t, n), BF16),
        compiler_params=_cparams("parallel"),
        name=name,
    )(h, w, g.reshape(1, HEAD_DIM), cosf, sinf)


def _proj_dt_kernel(h_ref, w_ref, wt_ref, o_ref, ot_ref):
    h = h_ref[...]
    o_ref[...] = _dot(h, w_ref[...])
    ot_ref[...] = _dot_nt(wt_ref[...], h)


def _proj_dt(h, w, bm=512):
    t, d = h.shape
    n = w.shape[1]
    return pl.pallas_call(
        _proj_dt_kernel,
        grid=(t // bm,),
        in_specs=[pl.BlockSpec((bm, d), lambda i: (i, 0)),
                  pl.BlockSpec((d, n), lambda i: (0, 0)),
                  pl.BlockSpec((n, d), lambda i: (0, 0))],
        out_specs=[pl.BlockSpec((bm, n), lambda i: (i, 0)),
                   pl.BlockSpec((n, bm), lambda i: (0, i))],
        out_shape=[jax.ShapeDtypeStruct((t, n), F32),
                   jax.ShapeDtypeStruct((n, t), F32)],
        compiler_params=_cparams("parallel"),
        name="proj_dt",
    )(h, w, w.T)


def _conv_kernel(x_ref, prev_ref, next_ref, w_ref, b_ref, xs_ref, b_out, bt_out, c_out, ext_ref,
                 *, bc):
    i = pl.program_id(1)
    nb = pl.num_programs(1)
    halo = SUBLANES
    pad = CONV_W // 2
    ext_ref[0:halo, :] = prev_ref[...] * (i > 0).astype(F32)
    ext_ref[halo:halo + bc, :] = x_ref[...]
    ext_ref[halo + bc:halo + bc + halo, :] = next_ref[...] * (i < nb - 1).astype(F32)
    acc = b_ref[...] + ext_ref[halo - pad:halo - pad + bc, :] * w_ref[0:1, :]
    for k in range(1, CONV_W):
        acc = acc + ext_ref[halo - pad + k:halo - pad + k + bc, :] * w_ref[k:k + 1, :]
    y = _silu(acc)
    xs_ref[...] = y[:, :D_INNER]
    bm = y[:, D_INNER:D_INNER + GN]
    b_out[...] = bm.astype(b_out.dtype)
    bt_out[...] = bm.T.astype(bt_out.dtype)
    c_out[...] = y[:, D_INNER + GN:].astype(c_out.dtype)


def _conv(xbc, conv_w, conv_b, bc=256):
    b, s, c = xbc.shape
    nb = s // bc
    hb = bc // SUBLANES
    last_hb = s // SUBLANES - 1
    return pl.pallas_call(
        functools.partial(_conv_kernel, bc=bc),
        grid=(b, nb),
        in_specs=[pl.BlockSpec((None, bc, c), lambda bi, i: (bi, i, 0)),
                  pl.BlockSpec((None, SUBLANES, c), lambda bi, i: (bi, jnp.maximum(i * hb - 1, 0), 0)),
                  pl.BlockSpec((None, SUBLANES, c), lambda bi, i: (bi, jnp.minimum((i + 1) * hb, last_hb), 0)),
                  pl.BlockSpec((CONV_W, c), lambda bi, i: (0, 0)),
                  pl.BlockSpec((1, c), lambda bi, i: (0, 0))],
        out_specs=[pl.BlockSpec((None, bc, D_INNER), lambda bi, i: (bi, i, 0)),
                   pl.BlockSpec((None, bc, GN), lambda bi, i: (bi, i, 0)),
                   pl.BlockSpec((None, GN, bc), lambda bi, i: (bi, 0, i)),
                   pl.BlockSpec((None, bc, GN), lambda bi, i: (bi, i, 0))],
        out_shape=[jax.ShapeDtypeStruct((b, s, D_INNER), F32),
                   jax.ShapeDtypeStruct((b, s, GN), BF16),
                   jax.ShapeDtypeStruct((b, GN, s), BF16),
                   jax.ShapeDtypeStruct((b, s, GN), BF16)],
        scratch_shapes=[pltpu.VMEM((bc + 2 * SUBLANES, c), F32)],
        compiler_params=_cparams("parallel", "parallel"),
        name="ssd_conv",
    )(xbc, xbc, xbc, conv_w, conv_b.reshape(1, c))


def _ssd_direction(direction, xs_ref, b_ref, bt_ref, c_ref, dt_ref, dtt_ref, bias_ref, biast_ref,
                   alog_ref, alogt_ref, dskip_ref, rep_ref, y_ref, state_ref):
    q = SSD_CHUNK
    hg = SSD_HEADS // SSD_GROUPS
    gw = hg * SSD_HEADDIM
    row = lax.broadcasted_iota(jnp.int32, (q, q), 0)
    col = lax.broadcasted_iota(jnp.int32, (q, q), 1)
    if direction == 0:
        keep = col <= row
        last = q - 1
    else:
        keep = col >= row
        last = 0
    tri = keep.astype(BF16)
    tri_t = (row <= col).astype(BF16) if direction == 0 else (row >= col).astype(BF16)

    dt = _softplus(dt_ref[...] + bias_ref[...])
    dtt = _softplus(dtt_ref[...] + biast_ref[...])
    dta = dt * (-jnp.exp(alog_ref[...]))
    dtat = dtt * (-jnp.exp(alogt_ref[...]))
    acs = _dot_exact_rhs(tri, dta)
    acst = _dot_exact_lhs(dtat, tri_t)

    rep = rep_ref[...]
    acs_e = _dot_exact_lhs(acs, rep)
    dt_e = _dot_exact_lhs(dt, rep)
    xs = xs_ref[...]
    xc = xs * dt_e
    xcb = xc.astype(BF16)
    last_e = acs_e[last:last + 1, :]
    xd = (xc * jnp.exp(last_e - acs_e)).astype(BF16)
    chunk_decay = jnp.exp(last_e)
    eacs = jnp.exp(acs_e)
    lane = lax.broadcasted_iota(jnp.int32, (q, LANES), 1)
    first_half = lane < SSD_HEADDIM

    for g in range(SSD_GROUPS):
        gs = slice(g * gw, (g + 1) * gw)
        ns = slice(g * SSD_STATE, (g + 1) * SSD_STATE)
        cg = c_ref[:, ns]
        bgt = bt_ref[ns, :]
        cb = _dot(cg, bgt)
        h_in = state_ref[:, gs]
        y_off = _dot(cg, h_in.astype(BF16)) * eacs[:, gs]
        st = _dot(bgt, xd[:, gs])
        state_ref[:, gs] = h_in * chunk_decay[:, gs] + st
        for jj in range(hg // 2):
            j = g * (hg // 2) + jj
            ms = []
            for hh in (2 * j, 2 * j + 1):
                ci = direction * SSD_HEADS + hh
                seg = acs[:, ci:ci + 1] - acst[ci:ci + 1, :]
                lm = jnp.exp(jnp.where(keep, seg, NEG_BIG))
                ms.append((cb * lm).astype(BF16))
            lhs = jnp.concatenate(ms, axis=1)
            x2 = xcb[:, j * LANES:(j + 1) * LANES]
            zero = jnp.zeros_like(x2)
            rhs = jnp.concatenate([jnp.where(first_half, x2, zero),
                                   jnp.where(first_half, zero, x2)], axis=0)
            y = _dot(lhs, rhs) + y_off[:, jj * LANES:(jj + 1) * LANES]
            if direction == 0:
                cs = slice(j * LANES, (j + 1) * LANES)
                y = y + xs[:, cs] * dskip_ref[:, cs]
            y_ref[:, j * LANES:(j + 1) * LANES] = y


def _ssd_kernel(xs_f, b_f, bt_f, c_f, dt_f, dtt_f, xs_b, b_b, bt_b, c_b, dt_b, dtt_b,
                bias_ref, biast_ref, alog_ref, alogt_ref, dskip_ref, rep_ref,
                yf_ref, yb_ref, sf_ref, sb_ref):
    @pl.when(pl.program_id(1) == 0)
    def _():
        sf_ref[...] = jnp.zeros_like(sf_ref)
        sb_ref[...] = jnp.zeros_like(sb_ref)

    _ssd_direction(0, xs_f, b_f, bt_f, c_f, dt_f, dtt_f, bias_ref, biast_ref, alog_ref, alogt_ref,
                   dskip_ref, rep_ref.at[0], yf_ref, sf_ref)
    _ssd_direction(1, xs_b, b_b, bt_b, c_b, dt_b, dtt_b, bias_ref, biast_ref, alog_ref, alogt_ref,
                   dskip_ref, rep_ref.at[1], yb_ref, sb_ref)


def _ssd_scan(xs, bm, bmt, cm, dt, dtt, dt_bias, a_log, d_skip):
    b, s, _ = xs.shape
    q = SSD_CHUNK
    nc = s // q
    nh2 = 2 * SSD_HEADS

    def fwd(bi, c):
        return c

    def bwd(bi, c):
        return nc - 1 - c

    def chunk_specs(cidx):
        return [pl.BlockSpec((None, q, D_INNER), lambda bi, c: (bi, cidx(bi, c), 0)),
                pl.BlockSpec((None, q, GN), lambda bi, c: (bi, cidx(bi, c), 0)),
                pl.BlockSpec((None, GN, q), lambda bi, c: (bi, 0, cidx(bi, c))),
                pl.BlockSpec((None, q, GN), lambda bi, c: (bi, cidx(bi, c), 0)),
                pl.BlockSpec((None, q, nh2), lambda bi, c: (bi, cidx(bi, c), 0)),
                pl.BlockSpec((nh2, q), lambda bi, c: (0, bi * nc + cidx(bi, c)))]

    def const(shape):
        return pl.BlockSpec(shape, lambda bi, c: (0,) * len(shape))

    rep = np.zeros((2, nh2, D_INNER), np.float32)
    for d in range(2):
        for h in range(SSD_HEADS):
            rep[d, d * SSD_HEADS + h, h * SSD_HEADDIM:(h + 1) * SSD_HEADDIM] = 1.0
    rep = jnp.asarray(rep, BF16)
    dskip_e = jnp.repeat(d_skip.astype(F32), SSD_HEADDIM).reshape(1, D_INNER)
    bias = dt_bias.reshape(1, nh2).astype(F32)
    alog = a_log.reshape(1, nh2).astype(F32)

    ins = (xs, bm, bmt, cm, dt.reshape(b, s, nh2), dtt)
    return pl.pallas_call(
        _ssd_kernel,
        grid=(b, nc),
        in_specs=chunk_specs(fwd) + chunk_specs(bwd) + [
            const((1, nh2)), const((nh2, 1)), const((1, nh2)), const((nh2, 1)),
            const((1, D_INNER)), const((2, nh2, D_INNER))],
        out_specs=[pl.BlockSpec((None, q, D_INNER), lambda bi, c: (bi, c, 0)),
                   pl.BlockSpec((None, q, D_INNER), lambda bi, c: (bi, nc - 1 - c, 0))],
        out_shape=[jax.ShapeDtypeStruct((b, s, D_INNER), F32),
                   jax.ShapeDtypeStruct((b, s, D_INNER), F32)],
        scratch_shapes=[pltpu.VMEM((SSD_STATE, D_INNER), F32),
                        pltpu.VMEM((SSD_STATE, D_INNER), F32)],
        compiler_params=_cparams("parallel", "arbitrary"),
        name="ssd_scan",
    )(*ins, *ins, bias, bias.reshape(nh2, 1), alog, alog.reshape(nh2, 1), dskip_e, rep)


def _flash_kernel(q_ref, k_ref, v_ref, o_ref, qs_ref, acc_ref, m_ref, l_ref, *, bq, bk):
    for r in range(GQA_REP):
        qs_ref[r * bq:(r + 1) * bq, :] = q_ref[:, r * HEAD_DIM:(r + 1) * HEAD_DIM]
    m_ref[...] = jnp.full_like(m_ref, NEG_BIG)
    l_ref[...] = jnp.zeros_like(l_ref)
    acc_ref[...] = jnp.zeros_like(acc_ref)
    nk = k_ref.shape[0] // bk

    def body(j, carry):
        off = pl.multiple_of(j * bk, bk)
        kb = k_ref[pl.ds(off, bk), :]
        vb = v_ref[pl.ds(off, bk), :]
        s = _dot_nt(qs_ref[...], kb)
        m_prev = m_ref[...]
        m_new = jnp.maximum(m_prev, jnp.max(s, axis=-1, keepdims=True))
        alpha = jnp.exp(m_prev - m_new)
        p = jnp.exp(s - m_new)
        l_ref[...] = alpha * l_ref[...] + jnp.sum(p, axis=-1, keepdims=True)
        acc_ref[...] = alpha * acc_ref[...] + _dot(p.astype(BF16), vb)
        m_ref[...] = m_new
        return carry

    lax.fori_loop(0, nk, body, 0)
    inv = 1.0 / l_ref[...]
    for r in range(GQA_REP):
        o_ref[:, r * HEAD_DIM:(r + 1) * HEAD_DIM] = (
            acc_ref[r * bq:(r + 1) * bq, :] * inv[r * bq:(r + 1) * bq, :]).astype(o_ref.dtype)


def _flash_gqa(q, k, v, bq=256, bk=1024):
    b, s, _ = q.shape
    bk = min(bk, s)
    rows = GQA_REP * bq
    gw = GQA_REP * HEAD_DIM
    return pl.pallas_call(
        functools.partial(_flash_kernel, bq=bq, bk=bk),
        grid=(b, GQA_KV_HEADS, s // bq),
        in_specs=[pl.BlockSpec((None, bq, gw), lambda bi, kh, i: (bi, i, kh)),
                  pl.BlockSpec((None, s, HEAD_DIM), lambda bi, kh, i: (bi, 0, kh)),
                  pl.BlockSpec((None, s, HEAD_DIM), lambda bi, kh, i: (bi, 0, kh))],
        out_specs=pl.BlockSpec((None, bq, gw), lambda bi, kh, i: (bi, i, kh)),
        out_shape=jax.ShapeDtypeStruct((b, s, GQA_WIDTH), BF16),
        scratch_shapes=[pltpu.VMEM((rows, HEAD_DIM), BF16),
                        pltpu.VMEM((rows, HEAD_DIM), F32),
                        pltpu.VMEM((rows, 1), F32),
                        pltpu.VMEM((rows, 1), F32)],
        compiler_params=_cparams("parallel", "parallel", "parallel"),
        name="flash_gqa",
    )(q, k, v)


def _band_kernel(q_ref, kp_ref, kc_ref, kn_ref, vp_ref, vc_ref, vn_ref, bias_ref, o_ref, lse_ref,
                 *, bq, length):
    i = pl.program_id(2)
    col = lax.broadcasted_iota(jnp.int32, (bq, 3 * bq), 1)
    kpos = i * bq - bq + col
    inside = (kpos >= 0) & (kpos < length)
    for hd in range(DIL_HEADS_PER_GROUP):
        sl = slice(hd * HEAD_DIM, (hd + 1) * HEAD_DIM)
        k3 = jnp.concatenate([kp_ref[:, sl], kc_ref[:, sl], kn_ref[:, sl]], axis=0)
        v3 = jnp.concatenate([vp_ref[:, sl], vc_ref[:, sl], vn_ref[:, sl]], axis=0)
        s = _dot_nt(q_ref[:, sl], k3) * ATTN_SCALE + bias_ref[hd]
        s = jnp.where(inside, s, NEG_BIG)
        m = jnp.max(s, axis=-1, keepdims=True)
        p = jnp.exp(s - m)
        l = jnp.sum(p, axis=-1, keepdims=True)
        o = _dot(p.astype(BF16), v3) / l
        o_ref[:, sl] = o
        lse_ref[:, sl] = jnp.broadcast_to(m + jnp.log(l), (bq, HEAD_DIM))


def _band_attention(dq, dk, dv, bias_t, group, dil, bq=128):
    b, s, w = dq.shape
    length = s // dil
    nb = length // bq
    ngroups = w // DIL_OUT
    view = lambda t: t.reshape(b, length, dil * w)
    qv, kv, vv = view(dq), view(dk), view(dv)

    def at(shift):
        def index(bi, r, i):
            return (bi, jnp.clip(i + shift, 0, nb - 1), r * ngroups + group)
        return pl.BlockSpec((None, bq, DIL_OUT), index)

    out_spec = pl.BlockSpec((None, bq, DIL_OUT), lambda bi, r, i: (bi, i, r))
    o, lse = pl.pallas_call(
        functools.partial(_band_kernel, bq=bq, length=length),
        grid=(b, dil, nb),
        in_specs=[at(0), at(-1), at(0), at(1), at(-1), at(0), at(1),
                  pl.BlockSpec((DIL_HEADS_PER_GROUP, bq, 3 * bq), lambda bi, r, i: (0, 0, 0))],
        out_specs=[out_spec, out_spec],
        out_shape=[jax.ShapeDtypeStruct((b, length, dil * DIL_OUT), F32),
                   jax.ShapeDtypeStruct((b, length, dil * DIL_OUT), F32)],
        compiler_params=_cparams("parallel", "parallel", "parallel"),
        name=f"band_attn_g{group}",
    )(qv, kv, kv, kv, vv, vv, vv, bias_t)
    return o.reshape(b, s, DIL_OUT), lse.reshape(b, s, DIL_OUT)


def _t5_bucket(rel):
    nb = N_BUCKETS // 2
    max_exact = nb // 2
    ret = jnp.where(rel > 0, nb, 0)
    n = jnp.abs(rel)
    nf = jnp.maximum(n, 1).astype(F32)
    large = max_exact + (jnp.log(nf / max_exact) / math.log(REL_MAX_DIST / max_exact)
                         * (nb - max_exact)).astype(jnp.int32)
    large = jnp.minimum(large, nb - 1)
    return ret + jnp.where(n < max_exact, n, large)


def _band_bias_tiles(rel_bias, bq=128):
    i = np.arange(bq)[:, None]
    j = np.arange(3 * bq)[None, :]
    rel = (j - bq) - i
    band = np.abs(rel) <= DIL_HALF
    idx = np.clip(rel + DIL_HALF, 0, 2 * DIL_HALF)
    tiles = []
    for g, (_, dil) in enumerate(DIL_PATTERNS):
        dist = jnp.arange(-DIL_HALF, DIL_HALF + 1, dtype=jnp.int32) * dil
        tbl = rel_bias.astype(F32)[_t5_bucket(dist)]
        tbl = tbl[:, g * DIL_HEADS_PER_GROUP:(g + 1) * DIL_HEADS_PER_GROUP].T
        tiles.append(jnp.where(band[None], tbl[:, idx], NEG_BIG))
    return tiles


def _merge_kernel(x_ref, yf_ref, yb_ref, zs_ref, gssd_ref, ygqa_ref,
                  o0_ref, l0_ref, o1_ref, l1_ref, o2_ref, l2_ref, gates_ref,
                  wssd_ref, wgqa_ref, wdil_ref, wout_ref, gpost_ref, o_ref):
    y = (yf_ref[...] + yb_ref[...]) * zs_ref[...]
    y_ssd = _rms(y, gssd_ref[...]).astype(BF16)
    l0, l1, l2 = l0_ref[...], l1_ref[...], l2_ref[...]
    mx = jnp.maximum(jnp.maximum(l0, l1), l2)
    e0, e1, e2 = jnp.exp(l0 - mx), jnp.exp(l1 - mx), jnp.exp(l2 - mx)
    y_dil = ((e0 * o0_ref[...] + e1 * o1_ref[...] + e2 * o2_ref[...]) / (e0 + e1 + e2)).astype(BF16)
    d = D_MODEL
    mix = gates_ref[:, 0:d] * _dot(y_ssd, wssd_ref[...])
    mix = mix + gates_ref[:, d:2 * d] * _dot(ygqa_ref[...], wgqa_ref[...])
    mix = mix + gates_ref[:, 2 * d:3 * d] * _dot(y_dil, wdil_ref[...])
    out = _dot(mix.astype(BF16), wout_ref[...])
    o_ref[...] = x_ref[...] + _rms(out, gpost_ref[...])


def _merge(x2d, yf, yb, zs, g_ssd, y_gqa, dil_parts, gates, w_ssd, w_gqa, w_dil, w_out, g_post, bm=256):
    t, d = x2d.shape
    row = lambda n: pl.BlockSpec((bm, n), lambda i: (i, 0))
    const = lambda a: pl.BlockSpec(a.shape, lambda i: (0, 0))
    (o0, l0), (o1, l1), (o2, l2) = dil_parts
    g_ssd = g_ssd.reshape(1, -1)
    g_post = g_post.reshape(1, -1)
    args = (x2d, yf, yb, zs, g_ssd, y_gqa, o0, l0, o1, l1, o2, l2, gates, w_ssd, w_gqa, w_dil, w_out, g_post)
    specs = [row(d), row(D_INNER), row(D_INNER), row(D_INNER), const(g_ssd), row(GQA_WIDTH)]
    specs += [row(DIL_OUT)] * 6
    specs += [row(N_BRANCHES * d), const(w_ssd), const(w_gqa), const(w_dil), const(w_out), const(g_post)]
    return pl.pallas_call(
        _merge_kernel,
        grid=(t // bm,),
        in_specs=specs,
        out_specs=row(d),
        out_shape=jax.ShapeDtypeStruct((t, d), F32),
        compiler_params=_cparams("parallel"),
        name="merge_out",
    )(*args)


def _ffn_kernel(x_ref, p_ref, gpre_ref, wg_ref, wu_ref, wd_ref, gpost_ref, gple_ref, wpg_ref, wp_ref,
                o_ref, *, nc):
    x = x_ref[...]
    h = _rms(x, gpre_ref[...]).astype(BF16)
    ff = jnp.zeros(x.shape, F32)
    for c in range(0, wg_ref.shape[1], nc):
        a = _silu(_dot(h, wg_ref[:, c:c + nc])) * _dot(h, wu_ref[:, c:c + nc])
        ff = ff + _dot(a.astype(BF16), wd_ref[c:c + nc, :])
    x = x + _rms(ff, gpost_ref[...])
    hn = _rms(x, gple_ref[...]).astype(BF16)
    gate = _sigmoid(_dot(hn, wpg_ref[...]))
    o_ref[...] = x + _dot(p_ref[...].astype(BF16), wp_ref[...]) * gate


def _ffn_ple(x2d, p2d, g_pre, w_gate, w_up, w_down, g_post, g_ple, w_ple_gate, w_ple, bm=256):
    t, d = x2d.shape
    row = lambda n: pl.BlockSpec((bm, n), lambda i: (i, 0))
    const = lambda a: pl.BlockSpec(a.shape, lambda i: (0, 0))
    g_pre, g_post, g_ple = (g.reshape(1, d) for g in (g_pre, g_post, g_ple))
    args = (x2d, p2d, g_pre, w_gate, w_up, w_down, g_post, g_ple, w_ple_gate, w_ple)
    specs = [row(d), row(PLE_DIM)] + [const(a) for a in args[2:]]
    nc = 256 if FFN_DIM % 256 == 0 else FFN_DIM
    return pl.pallas_call(
        functools.partial(_ffn_kernel, nc=nc),
        grid=(t // bm,),
        in_specs=specs,
        out_specs=row(d),
        out_shape=jax.ShapeDtypeStruct((t, d), F32),
        compiler_params=_cparams("parallel"),
        name="ffn_ple",
    )(*args)


def _rope_tables(seq):
    pos = jnp.arange(seq)
    row = (pos // GRID_W).astype(F32)
    colp = (pos % GRID_W).astype(F32)
    n_pairs = HEAD_DIM // 4
    inv = ROPE_THETA ** (-jnp.arange(n_pairs, dtype=F32) / n_pairs)
    ang = jnp.concatenate([row[:, None] * inv, colp[:, None] * inv], axis=-1)
    c, s = jnp.cos(ang), jnp.sin(ang)
    return jnp.concatenate([c, c], axis=-1), jnp.concatenate([-s, s], axis=-1)


def _layer(x, p_i, i, P, cosf, sinf, bias_tiles):
    b, s, d = x.shape
    t = b * s
    x2d = x.reshape(t, d)
    offs = np.concatenate([[0], np.cumsum(IN_WIDTHS)])
    w_in = P["w_in"][i]
    wz, wxbc, wdt, wq, wk, wv, wdq, wdk, wdv, wgates = (
        w_in[:, offs[j]:offs[j + 1]] for j in range(len(IN_WIDTHS)))
    perm = np.concatenate([np.arange(0, HEAD_DIM, 2), np.arange(1, HEAD_DIM, 2)])
    head_perm = lambda nh: (np.arange(nh)[:, None] * HEAD_DIM + perm[None, :]).reshape(-1)
    wq = wq[:, head_perm(GQA_Q_HEADS)]
    wk = wk[:, head_perm(GQA_KV_HEADS)]
    bf = lambda w: w.astype(BF16)

    h = _prenorm(x2d, P["g_pre_mix"][i])
    zs = _proj(h, bf(wz), F32, act="silu", name="proj_z")
    xbc = _proj(h, bf(wxbc), F32, name="proj_xbc")
    dt, dtt = _proj_dt(h, bf(wdt))
    q = _proj_qk(h, bf(wq), P["g_q"][i][perm], cosf, sinf, ATTN_SCALE, s, name="proj_q")
    k = _proj_qk(h, bf(wk), P["g_k"][i][perm], cosf, sinf, 1.0, s, name="proj_k")
    v = _proj(h, bf(wv), BF16, name="proj_v")
    dq = _proj(h, bf(wdq), BF16, name="proj_dq")
    dk = _proj(h, bf(wdk), BF16, name="proj_dk")
    dv = _proj(h, bf(wdv), BF16, name="proj_dv")
    gates = _proj(h, bf(wgates), F32, act="sigmoid", name="proj_gates")

    xs, bm, bmt, cm = _conv(xbc.reshape(b, s, XBC_DIM), P["conv_w"][i], P["conv_b"][i])
    yf, yb = _ssd_scan(xs, bm, bmt, cm, dt, dtt, P["dt_bias"][i], P["a_log"][i], P["d_skip"][i])

    y_gqa = _flash_gqa(q.reshape(b, s, GQA_WIDTH), k.reshape(b, s, GQA_KV_WIDTH),
                       v.reshape(b, s, GQA_KV_WIDTH))

    dil_parts = []
    for g, (_, dil) in enumerate(DIL_PATTERNS):
        o, lse = _band_attention(dq.reshape(b, s, DIL_WIDTH), dk.reshape(b, s, DIL_WIDTH),
                                 dv.reshape(b, s, DIL_WIDTH), bias_tiles[g], g, dil)
        dil_parts.append((o.reshape(t, DIL_OUT), lse.reshape(t, DIL_OUT)))

    x2d = _merge(x2d, yf.reshape(t, D_INNER), yb.reshape(t, D_INNER), zs, P["g_ssd"][i],
                 y_gqa.reshape(t, GQA_WIDTH), dil_parts, gates,
                 bf(P["w_br_ssd"][i]), bf(P["w_br_gqa"][i]), bf(P["w_br_dil"][i]), bf(P["w_out"][i]),
                 P["g_post_mix"][i])
    x2d = _ffn_ple(x2d, p_i.reshape(t, PLE_DIM), P["g_pre_ffn"][i], bf(P["w_gate"][i]), bf(P["w_up"][i]),
                   bf(P["w_down"][i]), P["g_post_ffn"][i], P["g_ple"][i], bf(P["w_ple_gate"][i]),
                   bf(P["w_ple"][i]))
    return x2d.reshape(b, s, d)


def _trunk(x, p, P, rel_bias):
    s = x.shape[1]
    cosf, sinf = _rope_tables(s)
    bias_tiles = _band_bias_tiles(rel_bias)
    for i in range(p.shape[0]):
        x = _layer(x, p[i], i, P, cosf, sinf, bias_tiles)
    return x


def kernel(x_prompt, x_sample, p_prompt, p_sample, w_in, conv_w, conv_b, dt_bias, a_log, d_skip, g_ssd, g_q, g_k, w_br_ssd, w_br_gqa, w_br_dil, w_out, g_pre_mix, g_post_mix, g_pre_ffn, g_post_ffn, w_gate, w_up, w_down, w_ple, g_ple, w_ple_gate, rel_bias):
    P = dict(w_in=w_in, conv_w=conv_w, conv_b=conv_b, dt_bias=dt_bias, a_log=a_log, d_skip=d_skip,
             g_ssd=g_ssd, g_q=g_q, g_k=g_k, w_br_ssd=w_br_ssd, w_br_gqa=w_br_gqa, w_br_dil=w_br_dil,
             w_out=w_out, g_pre_mix=g_pre_mix, g_post_mix=g_post_mix, g_pre_ffn=g_pre_ffn,
             g_post_ffn=g_post_ffn, w_gate=w_gate, w_up=w_up, w_down=w_down, w_ple=w_ple,
             g_ple=g_ple, w_ple_gate=w_ple_gate)
    nb = x_prompt.shape[0]
    if x_prompt.shape[1:] == x_sample.shape[1:]:
        x = jnp.concatenate([x_prompt, x_sample], axis=0)
        p = jnp.concatenate([p_prompt, p_sample], axis=1)
        y = _trunk(x, p, P, rel_bias)
        return (y[:nb], y[nb:])
    return (_trunk(x_prompt, p_prompt, P, rel_bias), _trunk(x_sample, p_sample, P, rel_bias))
```

```python
import functools
import math

import jax
import jax.numpy as jnp
import numpy as np
from jax import lax
from jax.experimental import pallas as pl
from jax.experimental.pallas import tpu as pltpu

F32 = jnp.float32
BF16 = jnp.bfloat16

D_MODEL = 1024
GRID_W = 64
PLE_DIM = 256
EPS = 1e-6
SSD_HEADS = 32
SSD_HEADDIM = 64
D_INNER = SSD_HEADS * SSD_HEADDIM
SSD_GROUPS = 4
SSD_STATE = 128
SSD_CHUNK = 128
CONV_W = 5
GN = SSD_GROUPS * SSD_STATE
XBC_DIM = D_INNER + 2 * GN
HEAD_DIM = 128
GQA_Q_HEADS = 16
GQA_KV_HEADS = 4
GQA_REP = GQA_Q_HEADS // GQA_KV_HEADS
GQA_WIDTH = GQA_Q_HEADS * HEAD_DIM
GQA_KV_WIDTH = GQA_KV_HEADS * HEAD_DIM
ROPE_THETA = 10000.0
ATTN_SCALE = HEAD_DIM ** -0.5
DIL_PATTERNS = ((128, 1), (512, 4), (2048, 16))
DIL_HEADS_PER_GROUP = 4
DIL_HEADS = len(DIL_PATTERNS) * DIL_HEADS_PER_GROUP
DIL_WIDTH = DIL_HEADS * HEAD_DIM
DIL_OUT = DIL_HEADS_PER_GROUP * HEAD_DIM
DIL_HALF = 64
BAND_TILE = 2 * DIL_HALF
BAND_TILES_PER_STEP = 16
N_BUCKETS = 32
REL_MAX_DIST = 2048
FFN_DIM = ((8 * D_MODEL // 3 + 255) // 256) * 256
N_BRANCHES = 3
IN_WIDTHS = (D_INNER, XBC_DIM, 2 * SSD_HEADS, GQA_WIDTH, GQA_KV_WIDTH, GQA_KV_WIDTH,
             DIL_WIDTH, DIL_WIDTH, DIL_WIDTH, N_BRANCHES * D_MODEL)

LANES = 128
SUBLANES = 8
VMEM_LIMIT = 56 * 1024 * 1024
NEG_BIG = -1e30
LOG2E = math.log2(math.e)
NORM_SLACK = 1.02
FAST_SOFTMAX_LOG2_RANGE = 100.0


def _cparams(*sem):
    return pltpu.CompilerParams(dimension_semantics=sem, vmem_limit_bytes=VMEM_LIMIT)


def _rms(x, g):
    ms = jnp.mean(x * x, axis=-1, keepdims=True)
    return x * lax.rsqrt(ms + EPS) * g


def _sigmoid(x):
    return 1.0 / (1.0 + jnp.exp(-x))


def _silu(x):
    return x * _sigmoid(x)


def _softplus(x):
    return jnp.maximum(x, 0.0) + jnp.log1p(jnp.exp(-jnp.abs(x)))


def _dot(a, b):
    return jnp.dot(a, b, preferred_element_type=F32)


def _dot_nt(a, b):
    return lax.dot_general(a, b, (((1,), (1,)), ((), ())), preferred_element_type=F32)


def _split3(x):
    hi = x.astype(BF16)
    r1 = x - hi.astype(F32)
    mid = r1.astype(BF16)
    lo = (r1 - mid.astype(F32)).astype(BF16)
    return hi, mid, lo


def _dot_exact_lhs(x, m_bf16, pieces=3):
    parts = _split3(x)[:pieces]
    acc = _dot(parts[0], m_bf16)
    for p in parts[1:]:
        acc = acc + _dot(p, m_bf16)
    return acc


def _dot_exact_rhs(m_bf16, x, pieces=3):
    parts = _split3(x)[:pieces]
    acc = _dot(m_bf16, parts[0])
    for p in parts[1:]:
        acc = acc + _dot(m_bf16, p)
    return acc


def _prenorm_kernel(x_ref, g_ref, o_ref):
    o_ref[...] = _rms(x_ref[...], g_ref[...]).astype(o_ref.dtype)


def _prenorm(x2d, g, bm=1024):
    t, d = x2d.shape
    return pl.pallas_call(
        _prenorm_kernel,
        grid=(t // bm,),
        in_specs=[pl.BlockSpec((bm, d), lambda i: (i, 0)),
                  pl.BlockSpec((1, d), lambda i: (0, 0))],
        out_specs=pl.BlockSpec((bm, d), lambda i: (i, 0)),
        out_shape=jax.ShapeDtypeStruct((t, d), BF16),
        compiler_params=_cparams("parallel"),
        name="prenorm",
    )(x2d, g.reshape(1, d))


def _proj_kernel(h_ref, w_ref, o_ref, *, act, nc):
    h = h_ref[...]
    n = o_ref.shape[-1]
    for c in range(0, n, nc):
        y = _dot(h, w_ref[:, c:c + nc])
        if act == "silu":
            y = _silu(y)
        elif act == "sigmoid":
            y = _sigmoid(y)
        o_ref[:, c:c + nc] = y.astype(o_ref.dtype)


def _proj(h, w, out_dtype, act=None, bm=512, name="proj"):
    t, d = h.shape
    n = w.shape[1]
    nc = 512 if n % 512 == 0 else n
    return pl.pallas_call(
        functools.partial(_proj_kernel, act=act, nc=nc),
        grid=(t // bm,),
        in_specs=[pl.BlockSpec((bm, d), lambda i: (i, 0)),
                  pl.BlockSpec((d, n), lambda i: (0, 0))],
        out_specs=pl.BlockSpec((bm, n), lambda i: (i, 0)),
        out_shape=jax.ShapeDtypeStruct((t, n), out_dtype),
        compiler_params=_cparams("parallel"),
        name=name,
    )(h, w)


def _proj_qk_kernel(h_ref, w_ref, g_ref, cos_ref, sin_ref, o_ref, *, scale):
    h = h_ref[...]
    g = g_ref[...]
    cosf = cos_ref[...]
    sinf = sin_ref[...]
    for hd in range(o_ref.shape[-1] // HEAD_DIM):
        sl = slice(hd * HEAD_DIM, (hd + 1) * HEAD_DIM)
        y = _rms(_dot(h, w_ref[:, sl]), g)
        y = y * cosf + pltpu.roll(y, HEAD_DIM // 2, 1) * sinf
        if scale != 1.0:
            y = y * scale
        o_ref[:, sl] = y.astype(o_ref.dtype)


def _proj_qk(h, w, g, cosf, sinf, scale, seq, bm=512, name="proj_qk"):
    t, d = h.shape
    n = w.shape[1]
    nsb = seq // bm
    return pl.pallas_call(
        functools.partial(_proj_qk_kernel, scale=scale),
        grid=(t // bm,),
        in_specs=[pl.BlockSpec((bm, d), lambda i: (i, 0)),
                  pl.BlockSpec((d, n), lambda i: (0, 0)),
                  pl.BlockSpec((1, HEAD_DIM), lambda i: (0, 0)),
                  pl.BlockSpec((bm, HEAD_DIM), lambda i: (i % nsb, 0)),
                  pl.BlockSpec((bm, HEAD_DIM), lambda i: (i % nsb, 0))],
        out_specs=pl.BlockSpec((bm, n), lambda i: (i, 0)),
        out_shape=jax.ShapeDtypeStruct((t, n), BF16),
        compiler_params=_cparams("parallel"),
        name=name,
    )(h, w, g.reshape(1, HEAD_DIM), cosf, sinf)


def _proj_dil_kernel(h_ref, w_ref, *refs):
    outs, scr = refs[:-1], refs[-1]
    h = h_ref[...]
    bm = h.shape[0]
    nlt = DIL_OUT // LANES
    for t in range(3):
        for g, (_, dil) in enumerate(DIL_PATTERNS):
            o_ref = outs[t * len(DIL_PATTERNS) + g]
            c0 = (t * len(DIL_PATTERNS) + g) * DIL_OUT
            y = _dot(h, w_ref[:, c0:c0 + DIL_OUT])
            if t == 0:
                y = y * ATTN_SCALE
            if dil == 1:
                o_ref[...] = y.astype(o_ref.dtype)
                continue
            for c in range(nlt):
                scr[c] = y[:, c * LANES:(c + 1) * LANES]
            for r in range(dil):
                for c in range(nlt):
                    o0 = r * DIL_OUT + c * LANES
                    o_ref[:, o0:o0 + LANES] = scr[c, pl.ds(r, bm // dil, stride=dil), :].astype(o_ref.dtype)


def _proj_dil(h, w, bm=512):
    t, d = h.shape
    n = w.shape[1]
    out_specs, out_shapes = [], []
    for _ in range(3):
        for _, dil in DIL_PATTERNS:
            out_specs.append(pl.BlockSpec((bm // dil, dil * DIL_OUT), lambda i: (i, 0)))
            out_shapes.append(jax.ShapeDtypeStruct((t // dil, dil * DIL_OUT), BF16))
    return pl.pallas_call(
        _proj_dil_kernel,
        grid=(t // bm,),
        in_specs=[pl.BlockSpec((bm, d), lambda i: (i, 0)),
                  pl.BlockSpec((d, n), lambda i: (0, 0))],
        out_specs=out_specs,
        out_shape=out_shapes,
        scratch_shapes=[pltpu.VMEM((DIL_OUT // LANES, bm, LANES), F32)],
        compiler_params=_cparams("parallel"),
        name="proj_dil",
    )(h, w)


def _proj_dt_kernel(h_ref, w_ref, wt_ref, o_ref, ot_ref):
    h = h_ref[...]
    o_ref[...] = _dot(h, w_ref[...])
    ot_ref[...] = _dot_nt(wt_ref[...], h)


def _proj_dt(h, w, bm=512):
    t, d = h.shape
    n = w.shape[1]
    return pl.pallas_call(
        _proj_dt_kernel,
        grid=(t // bm,),
        in_specs=[pl.BlockSpec((bm, d), lambda i: (i, 0)),
                  pl.BlockSpec((d, n), lambda i: (0, 0)),
                  pl.BlockSpec((n, d), lambda i: (0, 0))],
        out_specs=[pl.BlockSpec((bm, n), lambda i: (i, 0)),
                   pl.BlockSpec((n, bm), lambda i: (0, i))],
        out_shape=[jax.ShapeDtypeStruct((t, n), F32),
                   jax.ShapeDtypeStruct((n, t), F32)],
        compiler_params=_cparams("parallel"),
        name="proj_dt",
    )(h, w, w.T)


def _conv_kernel(x_ref, prev_ref, next_ref, w_ref, b_ref, xs_ref, b_out, bt_out, c_out, ext_ref,
                 *, bc):
    i = pl.program_id(1)
    nb = pl.num_programs(1)
    halo = SUBLANES
    pad = CONV_W // 2
    ext_ref[0:halo, :] = prev_ref[...] * (i > 0).astype(F32)
    ext_ref[halo:halo + bc, :] = x_ref[...]
    ext_ref[halo + bc:halo + bc + halo, :] = next_ref[...] * (i < nb - 1).astype(F32)
    acc = b_ref[...] + ext_ref[halo - pad:halo - pad + bc, :] * w_ref[0:1, :]
    for k in range(1, CONV_W):
        acc = acc + ext_ref[halo - pad + k:halo - pad + k + bc, :] * w_ref[k:k + 1, :]
    y = _silu(acc)
    xs_ref[...] = y[:, :D_INNER]
    bm = y[:, D_INNER:D_INNER + GN]
    b_out[...] = bm.astype(b_out.dtype)
    bt_out[...] = bm.T.astype(bt_out.dtype)
    c_out[...] = y[:, D_INNER + GN:].astype(c_out.dtype)


def _conv(xbc, conv_w, conv_b, bc=256):
    b, s, c = xbc.shape
    nb = s // bc
    hb = bc // SUBLANES
    last_hb = s // SUBLANES - 1
    return pl.pallas_call(
        functools.partial(_conv_kernel, bc=bc),
        grid=(b, nb),
        in_specs=[pl.BlockSpec((None, bc, c), lambda bi, i: (bi, i, 0)),
                  pl.BlockSpec((None, SUBLANES, c), lambda bi, i: (bi, jnp.maximum(i * hb - 1, 0), 0)),
                  pl.BlockSpec((None, SUBLANES, c), lambda bi, i: (bi, jnp.minimum((i + 1) * hb, last_hb), 0)),
                  pl.BlockSpec((CONV_W, c), lambda bi, i: (0, 0)),
                  pl.BlockSpec((1, c), lambda bi, i: (0, 0))],
        out_specs=[pl.BlockSpec((None, bc, D_INNER), lambda bi, i: (bi, i, 0)),
                   pl.BlockSpec((None, bc, GN), lambda bi, i: (bi, i, 0)),
                   pl.BlockSpec((None, GN, bc), lambda bi, i: (bi, 0, i)),
                   pl.BlockSpec((None, bc, GN), lambda bi, i: (bi, i, 0))],
        out_shape=[jax.ShapeDtypeStruct((b, s, D_INNER), F32),
                   jax.ShapeDtypeStruct((b, s, GN), BF16),
                   jax.ShapeDtypeStruct((b, GN, s), BF16),
                   jax.ShapeDtypeStruct((b, s, GN), BF16)],
        scratch_shapes=[pltpu.VMEM((bc + 2 * SUBLANES, c), F32)],
        compiler_params=_cparams("parallel", "parallel"),
        name="ssd_conv",
    )(xbc, xbc, xbc, conv_w, conv_b.reshape(1, c))


def _ssd_direction(direction, xs_ref, b_ref, bt_ref, c_ref, dt_ref, dtt_ref, bias_ref, biast_ref,
                   alog_ref, alogt_ref, dskip_ref, rep_ref, y_ref, state_ref):
    q = SSD_CHUNK
    hg = SSD_HEADS // SSD_GROUPS
    gw = hg * SSD_HEADDIM
    row = lax.broadcasted_iota(jnp.int32, (q, q), 0)
    col = lax.broadcasted_iota(jnp.int32, (q, q), 1)
    if direction == 0:
        keep = col <= row
        last = q - 1
    else:
        keep = col >= row
        last = 0
    tri = keep.astype(BF16)
    tri_t = (row <= col).astype(BF16) if direction == 0 else (row >= col).astype(BF16)

    dt = _softplus(dt_ref[...] + bias_ref[...])
    dtt = _softplus(dtt_ref[...] + biast_ref[...])
    dta = dt * (-jnp.exp(alog_ref[...]))
    dtat = dtt * (-jnp.exp(alogt_ref[...]))
    acs = _dot_exact_rhs(tri, dta)
    acst = _dot_exact_lhs(dtat, tri_t)

    rep = rep_ref[...]
    acs_e = _dot_exact_lhs(acs, rep)
    dt_e = _dot_exact_lhs(dt, rep)
    xs = xs_ref[...]
    xc = xs * dt_e
    xcb = xc.astype(BF16)
    last_e = acs_e[last:last + 1, :]
    xd = (xc * jnp.exp(last_e - acs_e)).astype(BF16)
    chunk_decay = jnp.exp(last_e)
    eacs = jnp.exp(acs_e)
    lane = lax.broadcasted_iota(jnp.int32, (q, LANES), 1)
    first_half = lane < SSD_HEADDIM

    for g in range(SSD_GROUPS):
        gs = slice(g * gw, (g + 1) * gw)
        ns = slice(g * SSD_STATE, (g + 1) * SSD_STATE)
        cg = c_ref[:, ns]
        bgt = bt_ref[ns, :]
        cb = _dot(cg, bgt)
        h_in = state_ref[:, gs]
        y_off = _dot(cg, h_in.astype(BF16)) * eacs[:, gs]
        st = _dot(bgt, xd[:, gs])
        state_ref[:, gs] = h_in * chunk_decay[:, gs] + st
        for jj in range(hg // 2):
            j = g * (hg // 2) + jj
            ms = []
            for hh in (2 * j, 2 * j + 1):
                ci = direction * SSD_HEADS + hh
                seg = acs[:, ci:ci + 1] - acst[ci:ci + 1, :]
                lm = jnp.exp(jnp.where(keep, seg, NEG_BIG))
                ms.append((cb * lm).astype(BF16))
            lhs = jnp.concatenate(ms, axis=1)
            x2 = xcb[:, j * LANES:(j + 1) * LANES]
            zero = jnp.zeros_like(x2)
            rhs = jnp.concatenate([jnp.where(first_half, x2, zero),
                                   jnp.where(first_half, zero, x2)], axis=0)
            y = _dot(lhs, rhs) + y_off[:, jj * LANES:(jj + 1) * LANES]
            if direction == 0:
                cs = slice(j * LANES, (j + 1) * LANES)
                y = y + xs[:, cs] * dskip_ref[:, cs]
            y_ref[:, j * LANES:(j + 1) * LANES] = y


def _ssd_kernel(xs_f, b_f, bt_f, c_f, dt_f, dtt_f, xs_b, b_b, bt_b, c_b, dt_b, dtt_b,
                bias_ref, biast_ref, alog_ref, alogt_ref, dskip_ref, rep_ref,
                yf_ref, yb_ref, sf_ref, sb_ref):
    @pl.when(pl.program_id(1) == 0)
    def _():
        sf_ref[...] = jnp.zeros_like(sf_ref)
        sb_ref[...] = jnp.zeros_like(sb_ref)

    _ssd_direction(0, xs_f, b_f, bt_f, c_f, dt_f, dtt_f, bias_ref, biast_ref, alog_ref, alogt_ref,
                   dskip_ref, rep_ref.at[0], yf_ref, sf_ref)
    _ssd_direction(1, xs_b, b_b, bt_b, c_b, dt_b, dtt_b, bias_ref, biast_ref, alog_ref, alogt_ref,
                   dskip_ref, rep_ref.at[1], yb_ref, sb_ref)


def _ssd_scan(xs, bm, bmt, cm, dt, dtt, dt_bias, a_log, d_skip):
    b, s, _ = xs.shape
    q = SSD_CHUNK
    nc = s // q
    nh2 = 2 * SSD_HEADS

    def fwd(bi, c):
        return c

    def bwd(bi, c):
        return nc - 1 - c

    def chunk_specs(cidx):
        return [pl.BlockSpec((None, q, D_INNER), lambda bi, c: (bi, cidx(bi, c), 0)),
                pl.BlockSpec((None, q, GN), lambda bi, c: (bi, cidx(bi, c), 0)),
                pl.BlockSpec((None, GN, q), lambda bi, c: (bi, 0, cidx(bi, c))),
                pl.BlockSpec((None, q, GN), lambda bi, c: (bi, cidx(bi, c), 0)),
                pl.BlockSpec((None, q, nh2), lambda bi, c: (bi, cidx(bi, c), 0)),
                pl.BlockSpec((nh2, q), lambda bi, c: (0, bi * nc + cidx(bi, c)))]

    def const(shape):
        return pl.BlockSpec(shape, lambda bi, c: (0,) * len(shape))

    rep = np.zeros((2, nh2, D_INNER), np.float32)
    for d in range(2):
        for h in range(SSD_HEADS):
            rep[d, d * SSD_HEADS + h, h * SSD_HEADDIM:(h + 1) * SSD_HEADDIM] = 1.0
    rep = jnp.asarray(rep, BF16)
    dskip_e = jnp.repeat(d_skip.astype(F32), SSD_HEADDIM).reshape(1, D_INNER)
    bias = dt_bias.reshape(1, nh2).astype(F32)
    alog = a_log.reshape(1, nh2).astype(F32)

    ins = (xs, bm, bmt, cm, dt.reshape(b, s, nh2), dtt)
    return pl.pallas_call(
        _ssd_kernel,
        grid=(b, nc),
        in_specs=chunk_specs(fwd) + chunk_specs(bwd) + [
            const((1, nh2)), const((nh2, 1)), const((1, nh2)), const((nh2, 1)),
            const((1, D_INNER)), const((2, nh2, D_INNER))],
        out_specs=[pl.BlockSpec((None, q, D_INNER), lambda bi, c: (bi, c, 0)),
                   pl.BlockSpec((None, q, D_INNER), lambda bi, c: (bi, nc - 1 - c, 0))],
        out_shape=[jax.ShapeDtypeStruct((b, s, D_INNER), F32),
                   jax.ShapeDtypeStruct((b, s, D_INNER), F32)],
        scratch_shapes=[pltpu.VMEM((SSD_STATE, D_INNER), F32),
                        pltpu.VMEM((SSD_STATE, D_INNER), F32)],
        compiler_params=_cparams("parallel", "arbitrary"),
        name="ssd_scan",
    )(*ins, *ins, bias, bias.reshape(nh2, 1), alog, alog.reshape(nh2, 1), dskip_e, rep)


def _flash_kernel(q_ref, k_ref, v_ref, o_ref, qa_ref, ka_ref, va_ref, kmax_ref, acc_ref, m_ref, l_ref,
                  *, bq, bk, unroll, bk_slow):
    seq = k_ref.shape[0]
    d = HEAD_DIM
    rows = GQA_REP * bq
    ones_sq = jnp.ones((d, d), BF16)
    first_lane = lax.broadcasted_iota(jnp.int32, (1, d), 1) == 0

    @pl.when(pl.program_id(2) == 0)
    def _():
        kmax_ref[...] = jnp.zeros_like(kmax_ref)
        one_col = jnp.where(first_lane, 1.0, 0.0).astype(BF16)

        def fill(c, carry):
            off = pl.multiple_of(c * bk_slow, bk_slow)
            kb = k_ref[pl.ds(off, bk_slow), :]
            ka_ref[pl.ds(off, bk_slow), 0:d] = kb
            ka_ref[pl.ds(off, bk_slow), d:2 * d] = jnp.broadcast_to(one_col, (bk_slow, d))
            va_ref[pl.ds(off, bk_slow), 0:d] = v_ref[pl.ds(off, bk_slow), :]
            va_ref[pl.ds(off, bk_slow), d:2 * d] = jnp.broadcast_to(one_col, (bk_slow, d))
            kf = kb.astype(F32)
            kn2 = _dot((kf * kf).astype(BF16), ones_sq)
            kmax_ref[...] = jnp.maximum(kmax_ref[...], jnp.max(kn2, axis=0, keepdims=True))
            return carry

        lax.fori_loop(0, seq // bk_slow, fill, 0)

    for r in range(GQA_REP):
        qa_ref[r * bq:(r + 1) * bq, 0:d] = q_ref[:, r * d:(r + 1) * d]
    qf = qa_ref[:, 0:d].astype(F32)
    qn2 = _dot((qf * qf).astype(BF16), ones_sq)
    shift = jnp.sqrt(qn2 * kmax_ref[...]) * NORM_SLACK
    qa_ref[:, d:2 * d] = jnp.where(first_lane, -shift, 0.0).astype(BF16)
    use_fast = jnp.max(shift) * 2.0 <= FAST_SOFTMAX_LOG2_RANGE

    def finish(acc, inv):
        for r in range(GQA_REP):
            rs = slice(r * bq, (r + 1) * bq)
            o_ref[:, r * d:(r + 1) * d] = (acc[rs, :] * inv[rs, :]).astype(o_ref.dtype)

    @pl.when(use_fast)
    def _():
        acc_ref[...] = jnp.zeros_like(acc_ref)

        def body(j, carry):
            qa = qa_ref[...]
            acc = acc_ref[...]
            for u in range(unroll):
                off = pl.multiple_of((j * unroll + u) * bk, bk)
                s = _dot_nt(qa, ka_ref[pl.ds(off, bk), :])
                acc = acc + _dot(jnp.exp2(s).astype(BF16), va_ref[pl.ds(off, bk), :])
            acc_ref[...] = acc
            return carry

        lax.fori_loop(0, seq // (bk * unroll), body, 0)
        acc = acc_ref[...]
        finish(acc[:, 0:d], 1.0 / acc[:, d:d + 1])

    @pl.when(jnp.logical_not(use_fast))
    def _():
        m_ref[...] = jnp.full_like(m_ref, NEG_BIG)
        l_ref[...] = jnp.zeros_like(l_ref)
        acc_ref[...] = jnp.zeros_like(acc_ref)

        def body(j, carry):
            off = pl.multiple_of(j * bk_slow, bk_slow)
            s = _dot_nt(qa_ref[:, 0:d], k_ref[pl.ds(off, bk_slow), :])
            m_prev = m_ref[...]
            m_new = jnp.maximum(m_prev, jnp.max(s, axis=-1, keepdims=True))
            alpha = jnp.exp2(m_prev - m_new)
            p = jnp.exp2(s - m_new)
            l_ref[...] = alpha * l_ref[...] + jnp.sum(p, axis=-1, keepdims=True)
            acc_ref[:, 0:d] = alpha * acc_ref[:, 0:d] + _dot(p.astype(BF16), v_ref[pl.ds(off, bk_slow), :])
            m_ref[...] = m_new
            return carry

        lax.fori_loop(0, seq // bk_slow, body, 0)
        finish(acc_ref[:, 0:d], 1.0 / l_ref[...])


def _flash_gqa(q, k, v, bq=256, bk=512, unroll=4, bk_slow=1024):
    b, s, _ = q.shape
    bk = min(bk, s)
    bk_slow = min(bk_slow, s)
    unroll = min(unroll, s // bk)
    rows = GQA_REP * bq
    gw = GQA_REP * HEAD_DIM
    return pl.pallas_call(
        functools.partial(_flash_kernel, bq=bq, bk=bk, unroll=unroll, bk_slow=bk_slow),
        grid=(b, GQA_KV_HEADS, s // bq),
        in_specs=[pl.BlockSpec((None, bq, gw), lambda bi, kh, i: (bi, i, kh)),
                  pl.BlockSpec((None, s, HEAD_DIM), lambda bi, kh, i: (bi, 0, kh)),
                  pl.BlockSpec((None, s, HEAD_DIM), lambda bi, kh, i: (bi, 0, kh))],
        out_specs=pl.BlockSpec((None, bq, gw), lambda bi, kh, i: (bi, i, kh)),
        out_shape=jax.ShapeDtypeStruct((b, s, GQA_WIDTH), BF16),
        scratch_shapes=[pltpu.VMEM((rows, 2 * HEAD_DIM), BF16),
                        pltpu.VMEM((s, 2 * HEAD_DIM), BF16),
                        pltpu.VMEM((s, 2 * HEAD_DIM), BF16),
                        pltpu.VMEM((1, HEAD_DIM), F32),
                        pltpu.VMEM((rows, 2 * HEAD_DIM), F32),
                        pltpu.VMEM((rows, 1), F32),
                        pltpu.VMEM((rows, 1), F32)],
        compiler_params=_cparams("parallel", "parallel", "arbitrary"),
        name="flash_gqa",
    )(q, k, v)


def _band_kernel(q_ref, kp_ref, kc_ref, kn_ref, vp_ref, vc_ref, vn_ref, bias_ref, o_ref, lse_ref,
                 *, dil, nsub, length):
    i = pl.program_id(1)
    tq = BAND_TILE
    col = lax.broadcasted_iota(jnp.int32, (tq, 2 * tq), 1)
    for u in range(nsub):
        kpos = (i * nsub + u) * tq - DIL_HALF + col
        inside = (kpos >= 0) & (kpos < length)
        rows = slice(u * tq, (u + 1) * tq)
        head = slice(u * tq - DIL_HALF, u * tq)
        tail = slice((u + 1) * tq, (u + 1) * tq + DIL_HALF)
        for r in range(dil):
            for hd in range(DIL_HEADS_PER_GROUP):
                c0 = (r * DIL_HEADS_PER_GROUP + hd) * HEAD_DIM
                sl = slice(c0, c0 + HEAD_DIM)

                def window(prev_ref, cur_ref, next_ref):
                    first = prev_ref[:, sl] if u == 0 else cur_ref[head, sl]
                    last = next_ref[:, sl] if u == nsub - 1 else cur_ref[tail, sl]
                    return jnp.concatenate([first, cur_ref[rows, sl], last], axis=0)

                s = _dot_nt(q_ref[rows, sl], window(kp_ref, kc_ref, kn_ref)) + bias_ref[hd]
                s = jnp.where(inside, s, NEG_BIG)
                m = jnp.max(s, axis=-1, keepdims=True)
                p = jnp.exp(s - m)
                l = jnp.sum(p, axis=-1, keepdims=True)
                o_ref[rows, sl] = _dot(p.astype(BF16), window(vp_ref, vc_ref, vn_ref)) / l
                lse_ref[rows, sl] = jnp.broadcast_to(m + jnp.log(l), (tq, HEAD_DIM))


def _band_attention(qr, kr, vr, bias_t, group, dil, batch):
    rows_total, width = qr.shape
    length = rows_total // batch
    nsub = max(1, BAND_TILES_PER_STEP // dil)
    bqr = nsub * BAND_TILE
    nb = length // bqr
    hb = bqr // DIL_HALF
    last_hb = length // DIL_HALF - 1
    view = lambda t: t.reshape(batch, length, width)

    cur = pl.BlockSpec((None, bqr, width), lambda bi, i: (bi, i, 0))
    prev = pl.BlockSpec((None, DIL_HALF, width), lambda bi, i: (bi, jnp.maximum(i * hb - 1, 0), 0))
    nxt = pl.BlockSpec((None, DIL_HALF, width), lambda bi, i: (bi, jnp.minimum((i + 1) * hb, last_hb), 0))
    o, lse = pl.pallas_call(
        functools.partial(_band_kernel, dil=dil, nsub=nsub, length=length),
        grid=(batch, nb),
        in_specs=[cur, prev, cur, nxt, prev, cur, nxt,
                  pl.BlockSpec((DIL_HEADS_PER_GROUP, BAND_TILE, 2 * BAND_TILE), lambda bi, i: (0, 0, 0))],
        out_specs=[cur, cur],
        out_shape=[jax.ShapeDtypeStruct((batch, length, width), F32),
                   jax.ShapeDtypeStruct((batch, length, width), F32)],
        compiler_params=_cparams("parallel", "parallel"),
        name=f"band_attn_g{group}",
    )(view(qr), view(kr), view(kr), view(kr), view(vr), view(vr), view(vr), bias_t)
    return o.reshape(rows_total, width), lse.reshape(rows_total, width)


def _t5_bucket(rel):
    nb = N_BUCKETS // 2
    max_exact = nb // 2
    ret = jnp.where(rel > 0, nb, 0)
    n = jnp.abs(rel)
    nf = jnp.maximum(n, 1).astype(F32)
    large = max_exact + (jnp.log(nf / max_exact) / math.log(REL_MAX_DIST / max_exact)
                         * (nb - max_exact)).astype(jnp.int32)
    large = jnp.minimum(large, nb - 1)
    return ret + jnp.where(n < max_exact, n, large)


def _band_bias_tiles(rel_bias):
    i = np.arange(BAND_TILE)[:, None]
    j = np.arange(2 * BAND_TILE)[None, :]
    rel = (j - DIL_HALF) - i
    band = np.abs(rel) <= DIL_HALF
    idx = np.clip(rel + DIL_HALF, 0, 2 * DIL_HALF)
    tiles = []
    for g, (_, dil) in enumerate(DIL_PATTERNS):
        dist = jnp.arange(-DIL_HALF, DIL_HALF + 1, dtype=jnp.int32) * dil
        tbl = rel_bias.astype(F32)[_t5_bucket(dist)]
        tbl = tbl[:, g * DIL_HEADS_PER_GROUP:(g + 1) * DIL_HEADS_PER_GROUP].T
        tiles.append(jnp.where(band[None], tbl[:, idx], NEG_BIG))
    return tiles


def _to_token_order(src_ref, scr_ref, dil):
    if dil == 1:
        return src_ref[...]
    n = src_ref.shape[0]
    nlt = DIL_OUT // LANES
    for r in range(dil):
        for c in range(nlt):
            c0 = r * DIL_OUT + c * LANES
            scr_ref[c, pl.ds(r, n, stride=dil), :] = src_ref[:, c0:c0 + LANES]
    return jnp.concatenate([scr_ref[c] for c in range(nlt)], axis=1)


def _merge_kernel(x_ref, yf_ref, yb_ref, zs_ref, gssd_ref, ygqa_ref,
                  o0_ref, l0_ref, o1_ref, l1_ref, o2_ref, l2_ref, gates_ref,
                  wssd_ref, wgqa_ref, wdil_ref, wout_ref, gpost_ref, o_ref, *scr):
    y = (yf_ref[...] + yb_ref[...]) * zs_ref[...]
    y_ssd = _rms(y, gssd_ref[...]).astype(BF16)
    dils = [dil for _, dil in DIL_PATTERNS]
    o0, o1, o2 = (_to_token_order(r, s, dl) for r, s, dl in zip((o0_ref, o1_ref, o2_ref), scr[0:3], dils))
    l0, l1, l2 = (_to_token_order(r, s, dl) for r, s, dl in zip((l0_ref, l1_ref, l2_ref), scr[3:6], dils))
    mx = jnp.maximum(jnp.maximum(l0, l1), l2)
    e0, e1, e2 = jnp.exp(l0 - mx), jnp.exp(l1 - mx), jnp.exp(l2 - mx)
    y_dil = ((e0 * o0 + e1 * o1 + e2 * o2) / (e0 + e1 + e2)).astype(BF16)
    d = D_MODEL
    mix = gates_ref[:, 0:d] * _dot(y_ssd, wssd_ref[...])
    mix = mix + gates_ref[:, d:2 * d] * _dot(ygqa_ref[...], wgqa_ref[...])
    mix = mix + gates_ref[:, 2 * d:3 * d] * _dot(y_dil, wdil_ref[...])
    out = _dot(mix.astype(BF16), wout_ref[...])
    o_ref[...] = x_ref[...] + _rms(out, gpost_ref[...])


def _merge(x2d, yf, yb, zs, g_ssd, y_gqa, dil_parts, gates, w_ssd, w_gqa, w_dil, w_out, g_post, bm=256):
    t, d = x2d.shape
    row = lambda n: pl.BlockSpec((bm, n), lambda i: (i, 0))
    const = lambda a: pl.BlockSpec(a.shape, lambda i: (0, 0))
    (o0, l0), (o1, l1), (o2, l2) = dil_parts
    g_ssd = g_ssd.reshape(1, -1)
    g_post = g_post.reshape(1, -1)
    args = (x2d, yf, yb, zs, g_ssd, y_gqa, o0, l0, o1, l1, o2, l2, gates, w_ssd, w_gqa, w_dil, w_out, g_post)
    specs = [row(d), row(D_INNER), row(D_INNER), row(D_INNER), const(g_ssd), row(GQA_WIDTH)]
    for _, dil in DIL_PATTERNS:
        specs += [pl.BlockSpec((bm // dil, dil * DIL_OUT), lambda i: (i, 0))] * 2
    specs += [row(N_BRANCHES * d), const(w_ssd), const(w_gqa), const(w_dil), const(w_out), const(g_post)]
    return pl.pallas_call(
        _merge_kernel,
        grid=(t // bm,),
        in_specs=specs,
        out_specs=row(d),
        out_shape=jax.ShapeDtypeStruct((t, d), F32),
        scratch_shapes=[pltpu.VMEM((DIL_OUT // LANES, bm, LANES), F32)] * 6,
        compiler_params=_cparams("parallel"),
        name="merge_out",
    )(*args)


def _ffn_kernel(x_ref, p_ref, gpre_ref, wg_ref, wu_ref, wd_ref, gpost_ref, gple_ref, wpg_ref, wp_ref,
                o_ref, *, nc):
    x = x_ref[...]
    h = _rms(x, gpre_ref[...]).astype(BF16)
    ff = jnp.zeros(x.shape, F32)
    for c in range(0, wg_ref.shape[1], nc):
        a = _silu(_dot(h, wg_ref[:, c:c + nc])) * _dot(h, wu_ref[:, c:c + nc])
        ff = ff + _dot(a.astype(BF16), wd_ref[c:c + nc, :])
    x = x + _rms(ff, gpost_ref[...])
    hn = _rms(x, gple_ref[...]).astype(BF16)
    gate = _sigmoid(_dot(hn, wpg_ref[...]))
    o_ref[...] = x + _dot(p_ref[...].astype(BF16), wp_ref[...]) * gate


def _ffn_ple(x2d, p2d, g_pre, w_gate, w_up, w_down, g_post, g_ple, w_ple_gate, w_ple, bm=256):
    t, d = x2d.shape
    row = lambda n: pl.BlockSpec((bm, n), lambda i: (i, 0))
    const = lambda a: pl.BlockSpec(a.shape, lambda i: (0, 0))
    g_pre, g_post, g_ple = (g.reshape(1, d) for g in (g_pre, g_post, g_ple))
    args = (x2d, p2d, g_pre, w_gate, w_up, w_down, g_post, g_ple, w_ple_gate, w_ple)
    specs = [row(d), row(PLE_DIM)] + [const(a) for a in args[2:]]
    nc = 256 if FFN_DIM % 256 == 0 else FFN_DIM
    return pl.pallas_call(
        functools.partial(_ffn_kernel, nc=nc),
        grid=(t // bm,),
        in_specs=specs,
        out_specs=row(d),
        out_shape=jax.ShapeDtypeStruct((t, d), F32),
        compiler_params=_cparams("parallel"),
        name="ffn_ple",
    )(*args)


def _rope_tables(seq):
    pos = jnp.arange(seq)
    row = (pos // GRID_W).astype(F32)
    colp = (pos % GRID_W).astype(F32)
    n_pairs = HEAD_DIM // 4
    inv = ROPE_THETA ** (-jnp.arange(n_pairs, dtype=F32) / n_pairs)
    ang = jnp.concatenate([row[:, None] * inv, colp[:, None] * inv], axis=-1)
    c, s = jnp.cos(ang), jnp.sin(ang)
    return jnp.concatenate([c, c], axis=-1), jnp.concatenate([-s, s], axis=-1)


def _layer(x, p_i, i, P, cosf, sinf, bias_tiles):
    b, s, d = x.shape
    t = b * s
    x2d = x.reshape(t, d)
    offs = np.concatenate([[0], np.cumsum(IN_WIDTHS)])
    w_in = P["w_in"][i]
    wz, wxbc, wdt, wq, wk, wv, wdq, wdk, wdv, wgates = (
        w_in[:, offs[j]:offs[j + 1]] for j in range(len(IN_WIDTHS)))
    def deinterleave(w):
        lead = w.shape[:-1]
        return w.reshape(*lead, -1, HEAD_DIM // 2, 2).swapaxes(-1, -2).reshape(*lead, -1)

    wq, wk = deinterleave(wq), deinterleave(wk)
    bf = lambda w: w.astype(BF16)

    h = _prenorm(x2d, P["g_pre_mix"][i])
    zs = _proj(h, bf(wz), F32, act="silu", name="proj_z")
    xbc = _proj(h, bf(wxbc), F32, name="proj_xbc")
    dt, dtt = _proj_dt(h, bf(wdt))
    q = _proj_qk(h, bf(wq), deinterleave(P["g_q"][i]), cosf, sinf, ATTN_SCALE * LOG2E, s, name="proj_q")
    k = _proj_qk(h, bf(wk), deinterleave(P["g_k"][i]), cosf, sinf, 1.0, s, name="proj_k")
    v = _proj(h, bf(wv), BF16, name="proj_v")
    dil_qkv = _proj_dil(h, bf(jnp.concatenate([wdq, wdk, wdv], axis=1)))
    gates = _proj(h, bf(wgates), F32, act="sigmoid", name="proj_gates")

    xs, bm, bmt, cm = _conv(xbc.reshape(b, s, XBC_DIM), P["conv_w"][i], P["conv_b"][i])
    yf, yb = _ssd_scan(xs, bm, bmt, cm, dt, dtt, P["dt_bias"][i], P["a_log"][i], P["d_skip"][i])

    y_gqa = _flash_gqa(q.reshape(b, s, GQA_WIDTH), k.reshape(b, s, GQA_KV_WIDTH),
                       v.reshape(b, s, GQA_KV_WIDTH))

    dil_parts = []
    ng = len(DIL_PATTERNS)
    for g, (_, dil) in enumerate(DIL_PATTERNS):
        dil_parts.append(_band_attention(dil_qkv[g], dil_qkv[ng + g], dil_qkv[2 * ng + g],
                                         bias_tiles[g], g, dil, b))

    x2d = _merge(x2d, yf.reshape(t, D_INNER), yb.reshape(t, D_INNER), zs, P["g_ssd"][i],
                 y_gqa.reshape(t, GQA_WIDTH), dil_parts, gates,
                 bf(P["w_br_ssd"][i]), bf(P["w_br_gqa"][i]), bf(P["w_br_dil"][i]), bf(P["w_out"][i]),
                 P["g_post_mix"][i])
    x2d = _ffn_ple(x2d, p_i.reshape(t, PLE_DIM), P["g_pre_ffn"][i], bf(P["w_gate"][i]), bf(P["w_up"][i]),
                   bf(P["w_down"][i]), P["g_post_ffn"][i], P["g_ple"][i], bf(P["w_ple_gate"][i]),
                   bf(P["w_ple"][i]))
    return x2d.reshape(b, s, d)


def _trunk(x, p, P, rel_bias):
    s = x.shape[1]
    cosf, sinf = _rope_tables(s)
    bias_tiles = _band_bias_tiles(rel_bias)
    for i in range(p.shape[0]):
        x = _layer(x, p[i], i, P, cosf, sinf, bias_tiles)
    return x


def kernel(x_prompt, x_sample, p_prompt, p_sample, w_in, conv_w, conv_b, dt_bias, a_log, d_skip, g_ssd, g_q, g_k, w_br_ssd, w_br_gqa, w_br_dil, w_out, g_pre_mix, g_post_mix, g_pre_ffn, g_post_ffn, w_gate, w_up, w_down, w_ple, g_ple, w_ple_gate, rel_bias):
    P = dict(w_in=w_in, conv_w=conv_w, conv_b=conv_b, dt_bias=dt_bias, a_log=a_log, d_skip=d_skip,
             g_ssd=g_ssd, g_q=g_q, g_k=g_k, w_br_ssd=w_br_ssd, w_br_gqa=w_br_gqa, w_br_dil=w_br_dil,
             w_out=w_out, g_pre_mix=g_pre_mix, g_post_mix=g_post_mix, g_pre_ffn=g_pre_ffn,
             g_post_ffn=g_post_ffn, w_gate=w_gate, w_up=w_up, w_down=w_down, w_ple=w_ple,
             g_ple=g_ple, w_ple_gate=w_ple_gate)
    nb = x_prompt.shape[0]
    if x_prompt.shape[1:] == x_sample.shape[1:]:
        x = jnp.concatenate([x_prompt, x_sample], axis=0)
        p = jnp.concatenate([p_prompt, p_sample], axis=1)
        y = _trunk(x, p, P, rel_bias)
        return (y[:nb], y[nb:])
    return (_trunk(x_prompt, p_prompt, P, rel_bias), _trunk(x_sample, p_sample, P, rel_bias))
```

```python
import functools
import math

import jax
import jax.numpy as jnp
import numpy as np
from jax import lax
from jax.experimental import pallas as pl
from jax.experimental.pallas import tpu as pltpu

F32 = jnp.float32
BF16 = jnp.bfloat16

D_MODEL = 1024
GRID_W = 64
PLE_DIM = 256
EPS = 1e-6
SSD_HEADS = 32
SSD_HEADDIM = 64
D_INNER = SSD_HEADS * SSD_HEADDIM
SSD_GROUPS = 4
SSD_STATE = 128
SSD_CHUNK = 128
CONV_W = 5
GN = SSD_GROUPS * SSD_STATE
XBC_DIM = D_INNER + 2 * GN
HEAD_DIM = 128
GQA_Q_HEADS = 16
GQA_KV_HEADS = 4
GQA_REP = GQA_Q_HEADS // GQA_KV_HEADS
GQA_WIDTH = GQA_Q_HEADS * HEAD_DIM
GQA_KV_WIDTH = GQA_KV_HEADS * HEAD_DIM
ROPE_THETA = 10000.0
ATTN_SCALE = HEAD_DIM ** -0.5
DIL_PATTERNS = ((128, 1), (512, 4), (2048, 16))
DIL_HEADS_PER_GROUP = 4
DIL_HEADS = len(DIL_PATTERNS) * DIL_HEADS_PER_GROUP
DIL_WIDTH = DIL_HEADS * HEAD_DIM
DIL_OUT = DIL_HEADS_PER_GROUP * HEAD_DIM
DIL_HALF = 64
BAND_TILE = 2 * DIL_HALF
BAND_TILES_PER_STEP = 16
N_BUCKETS = 32
REL_MAX_DIST = 2048
FFN_DIM = ((8 * D_MODEL // 3 + 255) // 256) * 256
N_BRANCHES = 3
IN_WIDTHS = (D_INNER, XBC_DIM, 2 * SSD_HEADS, GQA_WIDTH, GQA_KV_WIDTH, GQA_KV_WIDTH,
             DIL_WIDTH, DIL_WIDTH, DIL_WIDTH, N_BRANCHES * D_MODEL)

LANES = 128
SUBLANES = 8
BF16_ROWS = 16
MXU_COLS = 256
VMEM_LIMIT = 56 * 1024 * 1024
NEG_BIG = -1e30
LOG2E = math.log2(math.e)
NORM_SLACK = 1.02
FAST_SOFTMAX_LOG2_RANGE = 100.0


def _cparams(*sem):
    return pltpu.CompilerParams(dimension_semantics=sem, vmem_limit_bytes=VMEM_LIMIT)


def _rms(x, g):
    ms = jnp.mean(x * x, axis=-1, keepdims=True)
    return x * lax.rsqrt(ms + EPS) * g


def _sigmoid(x):
    return 1.0 / (1.0 + jnp.exp(-x))


def _silu(x):
    return x * _sigmoid(x)


def _softplus(x):
    return jnp.maximum(x, 0.0) + jnp.log1p(jnp.exp(-jnp.abs(x)))


def _dot(a, b):
    return jnp.dot(a, b, preferred_element_type=F32)


def _dot_nt(a, b):
    return lax.dot_general(a, b, (((1,), (1,)), ((), ())), preferred_element_type=F32)


def _split3(x):
    hi = x.astype(BF16)
    r1 = x - hi.astype(F32)
    mid = r1.astype(BF16)
    lo = (r1 - mid.astype(F32)).astype(BF16)
    return hi, mid, lo


def _dot_exact_lhs(x, m_bf16, pieces=3):
    parts = _split3(x)[:pieces]
    acc = _dot(parts[0], m_bf16)
    for p in parts[1:]:
        acc = acc + _dot(p, m_bf16)
    return acc


def _dot_exact_rhs(m_bf16, x, pieces=3):
    parts = _split3(x)[:pieces]
    acc = _dot(m_bf16, parts[0])
    for p in parts[1:]:
        acc = acc + _dot(m_bf16, p)
    return acc


def _prenorm_kernel(x_ref, g_ref, o_ref):
    o_ref[...] = _rms(x_ref[...], g_ref[...]).astype(o_ref.dtype)


def _prenorm(x2d, g, bm=1024):
    t, d = x2d.shape
    return pl.pallas_call(
        _prenorm_kernel,
        grid=(t // bm,),
        in_specs=[pl.BlockSpec((bm, d), lambda i: (i, 0)),
                  pl.BlockSpec((1, d), lambda i: (0, 0))],
        out_specs=pl.BlockSpec((bm, d), lambda i: (i, 0)),
        out_shape=jax.ShapeDtypeStruct((t, d), BF16),
        compiler_params=_cparams("parallel"),
        name="prenorm",
    )(x2d, g.reshape(1, d))


def _proj_kernel(h_ref, w_ref, o_ref, *, act, nc):
    h = h_ref[...]
    n = o_ref.shape[-1]
    for c in range(0, n, nc):
        y = _dot(h, w_ref[:, c:c + nc])
        if act == "silu":
            y = _silu(y)
        elif act == "sigmoid":
            y = _sigmoid(y)
        o_ref[:, c:c + nc] = y.astype(o_ref.dtype)


def _proj(h, w, out_dtype, act=None, bm=512, name="proj"):
    t, d = h.shape
    n = w.shape[1]
    nc = 512 if n % 512 == 0 else n
    return pl.pallas_call(
        functools.partial(_proj_kernel, act=act, nc=nc),
        grid=(t // bm,),
        in_specs=[pl.BlockSpec((bm, d), lambda i: (i, 0)),
                  pl.BlockSpec((d, n), lambda i: (0, 0))],
        out_specs=pl.BlockSpec((bm, n), lambda i: (i, 0)),
        out_shape=jax.ShapeDtypeStruct((t, n), out_dtype),
        compiler_params=_cparams("parallel"),
        name=name,
    )(h, w)


def _proj_qk_kernel(h_ref, w_ref, g_ref, cos_ref, sin_ref, o_ref, *, scale):
    h = h_ref[...]
    g = g_ref[...]
    cosf = cos_ref[...]
    sinf = sin_ref[...]
    for hd in range(o_ref.shape[-1] // HEAD_DIM):
        sl = slice(hd * HEAD_DIM, (hd + 1) * HEAD_DIM)
        y = _rms(_dot(h, w_ref[:, sl]), g)
        y = y * cosf + pltpu.roll(y, HEAD_DIM // 2, 1) * sinf
        if scale != 1.0:
            y = y * scale
        o_ref[:, sl] = y.astype(o_ref.dtype)


def _proj_qk(h, w, g, cosf, sinf, scale, seq, bm=512, name="proj_qk"):
    t, d = h.shape
    n = w.shape[1]
    nsb = seq // bm
    return pl.pallas_call(
        functools.partial(_proj_qk_kernel, scale=scale),
        grid=(t // bm,),
        in_specs=[pl.BlockSpec((bm, d), lambda i: (i, 0)),
                  pl.BlockSpec((d, n), lambda i: (0, 0)),
                  pl.BlockSpec((1, HEAD_DIM), lambda i: (0, 0)),
                  pl.BlockSpec((bm, HEAD_DIM), lambda i: (i % nsb, 0)),
                  pl.BlockSpec((bm, HEAD_DIM), lambda i: (i % nsb, 0))],
        out_specs=pl.BlockSpec((bm, n), lambda i: (i, 0)),
        out_shape=jax.ShapeDtypeStruct((t, n), BF16),
        compiler_params=_cparams("parallel"),
        name=name,
    )(h, w, g.reshape(1, HEAD_DIM), cosf, sinf)


def _proj_dil_kernel(h_ref, w_ref, *refs):
    outs, scr = refs[:-1], refs[-1]
    h = h_ref[...]
    bm = h.shape[0]
    nlt = DIL_OUT // LANES
    for t in range(3):
        for g, (_, dil) in enumerate(DIL_PATTERNS):
            o_ref = outs[t * len(DIL_PATTERNS) + g]
            c0 = (t * len(DIL_PATTERNS) + g) * DIL_OUT
            y = _dot(h, w_ref[:, c0:c0 + DIL_OUT])
            if t == 0:
                y = y * ATTN_SCALE
            if dil == 1:
                o_ref[...] = y.astype(o_ref.dtype)
                continue
            for c in range(nlt):
                scr[c] = y[:, c * LANES:(c + 1) * LANES]
            for r in range(dil):
                for c in range(nlt):
                    o0 = r * DIL_OUT + c * LANES
                    o_ref[:, o0:o0 + LANES] = scr[c, pl.ds(r, bm // dil, stride=dil), :].astype(o_ref.dtype)


def _proj_dil(h, w, bm=512):
    t, d = h.shape
    n = w.shape[1]
    out_specs, out_shapes = [], []
    for _ in range(3):
        for _, dil in DIL_PATTERNS:
            out_specs.append(pl.BlockSpec((bm // dil, dil * DIL_OUT), lambda i: (i, 0)))
            out_shapes.append(jax.ShapeDtypeStruct((t // dil, dil * DIL_OUT), BF16))
    return pl.pallas_call(
        _proj_dil_kernel,
        grid=(t // bm,),
        in_specs=[pl.BlockSpec((bm, d), lambda i: (i, 0)),
                  pl.BlockSpec((d, n), lambda i: (0, 0))],
        out_specs=out_specs,
        out_shape=out_shapes,
        scratch_shapes=[pltpu.VMEM((DIL_OUT // LANES, bm, LANES), F32)],
        compiler_params=_cparams("parallel"),
        name="proj_dil",
    )(h, w)


def _proj_dt_kernel(h_ref, w_ref, wt_ref, o_ref, ot_ref):
    h = h_ref[...]
    o_ref[...] = _dot(h, w_ref[...])
    ot_ref[...] = _dot_nt(wt_ref[...], h)


def _proj_dt(h, w, bm=512):
    t, d = h.shape
    n = w.shape[1]
    return pl.pallas_call(
        _proj_dt_kernel,
        grid=(t // bm,),
        in_specs=[pl.BlockSpec((bm, d), lambda i: (i, 0)),
                  pl.BlockSpec((d, n), lambda i: (0, 0)),
                  pl.BlockSpec((n, d), lambda i: (0, 0))],
        out_specs=[pl.BlockSpec((bm, n), lambda i: (i, 0)),
                   pl.BlockSpec((n, bm), lambda i: (0, i))],
        out_shape=[jax.ShapeDtypeStruct((t, n), F32),
                   jax.ShapeDtypeStruct((n, t), F32)],
        compiler_params=_cparams("parallel"),
        name="proj_dt",
    )(h, w, w.T)


def _proj_conv_kernel(h_ref, hp_ref, hn_ref, w_ref, cw_ref, cb_ref, xs_ref, b_out, bt_out, c_out, *ext_refs,
                      nsb):
    i = pl.program_id(0)
    bm = h_ref.shape[0]
    n = w_ref.shape[1]
    halo = SUBLANES
    pad = CONV_W // 2
    keep_prev = (i % nsb > 0).astype(F32)
    keep_next = (i % nsb < nsb - 1).astype(F32)
    nc = ext_refs[0].shape[1]
    for ext_ref, c in zip(ext_refs, range(0, n, nc)):
        w = w_ref[:, c:c + nc]
        ext_ref[halo:halo + bm, :] = _dot(h_ref[...], w)
        ext_ref[0:halo, :] = _dot(hp_ref[...], w)[BF16_ROWS - halo:, :] * keep_prev
        ext_ref[halo + bm:halo + bm + halo, :] = _dot(hn_ref[...], w)[0:halo, :] * keep_next
    rc = LANES
    for c in range(0, n, LANES):
        cs = slice(c, c + LANES)
        ext_ref = ext_refs[c // nc]
        es = slice(c % nc, c % nc + LANES)
        taps = [cw_ref[k:k + 1, cs] for k in range(CONV_W)]
        bias = cb_ref[:, cs]
        for r in range(0, bm, rc):
            base = halo - pad + r
            acc = bias + ext_ref[base:base + rc, es] * taps[0]
            for k in range(1, CONV_W):
                acc = acc + ext_ref[base + k:base + k + rc, es] * taps[k]
            y = _silu(acc)
            rs = slice(r, r + rc)
            if c < D_INNER:
                xs_ref[rs, cs] = y
            elif c < D_INNER + GN:
                cc = slice(c - D_INNER, c - D_INNER + LANES)
                b_out[rs, cc] = y.astype(b_out.dtype)
                bt_out[cc, rs] = y.T.astype(bt_out.dtype)
            else:
                cc = slice(c - D_INNER - GN, c - D_INNER - GN + LANES)
                c_out[rs, cc] = y.astype(c_out.dtype)


def _proj_conv(h, w, conv_w, conv_b, batch, bm=512):
    t, d = h.shape
    n = w.shape[1]
    s = t // batch
    nsb = s // bm
    hb = bm // BF16_ROWS
    last_hb = t // BF16_ROWS - 1
    nc = 2 * MXU_COLS
    return pl.pallas_call(
        functools.partial(_proj_conv_kernel, nsb=nsb),
        grid=(t // bm,),
        in_specs=[pl.BlockSpec((bm, d), lambda i: (i, 0)),
                  pl.BlockSpec((BF16_ROWS, d), lambda i: (jnp.maximum(i * hb - 1, 0), 0)),
                  pl.BlockSpec((BF16_ROWS, d), lambda i: (jnp.minimum((i + 1) * hb, last_hb), 0)),
                  pl.BlockSpec((d, n), lambda i: (0, 0)),
                  pl.BlockSpec((CONV_W, n), lambda i: (0, 0)),
                  pl.BlockSpec((1, n), lambda i: (0, 0))],
        out_specs=[pl.BlockSpec((bm, D_INNER), lambda i: (i, 0)),
                   pl.BlockSpec((bm, GN), lambda i: (i, 0)),
                   pl.BlockSpec((None, GN, bm), lambda i: (i // nsb, 0, i % nsb)),
                   pl.BlockSpec((bm, GN), lambda i: (i, 0))],
        out_shape=[jax.ShapeDtypeStruct((t, D_INNER), F32),
                   jax.ShapeDtypeStruct((t, GN), BF16),
                   jax.ShapeDtypeStruct((batch, GN, s), BF16),
                   jax.ShapeDtypeStruct((t, GN), BF16)],
        scratch_shapes=[pltpu.VMEM((bm + 2 * SUBLANES, nc), F32)] * (n // nc),
        compiler_params=_cparams("parallel"),
        name="proj_conv",
    )(h, h, h, w, conv_w, conv_b.reshape(1, n))


def _ssd_direction(direction, xs_ref, b_ref, bt_ref, c_ref, dt_ref, dtt_ref, bias_ref, biast_ref,
                   alog_ref, alogt_ref, dskip_ref, rep_ref, y_ref, state_ref):
    q = SSD_CHUNK
    hg = SSD_HEADS // SSD_GROUPS
    gw = hg * SSD_HEADDIM
    row = lax.broadcasted_iota(jnp.int32, (q, q), 0)
    col = lax.broadcasted_iota(jnp.int32, (q, q), 1)
    if direction == 0:
        keep = col <= row
        last = q - 1
    else:
        keep = col >= row
        last = 0
    tri = keep.astype(BF16)
    tri_t = (row <= col).astype(BF16) if direction == 0 else (row >= col).astype(BF16)

    dt = _softplus(dt_ref[...] + bias_ref[...])
    dtt = _softplus(dtt_ref[...] + biast_ref[...])
    dta = dt * (-jnp.exp(alog_ref[...]))
    dtat = dtt * (-jnp.exp(alogt_ref[...]))
    acs = _dot_exact_rhs(tri, dta)
    acst = _dot_exact_lhs(dtat, tri_t)

    rep3 = rep_ref[...]

    def expand(x):
        wide = jnp.concatenate([x, x, x, x], axis=1)
        hi, mid, lo = _split3(wide)
        lane = lax.broadcasted_iota(jnp.int32, wide.shape, 1)
        packed = jnp.where(lane < x.shape[1], hi, jnp.where(lane < 2 * x.shape[1], mid, lo))
        return _dot(packed, rep3)

    acs_e = expand(acs)
    dt_e = expand(dt)
    xs = xs_ref[...]
    xc = xs * dt_e
    xcb = xc.astype(BF16)
    last_e = acs_e[last:last + 1, :]
    xd = (xc * jnp.exp(last_e - acs_e)).astype(BF16)
    chunk_decay = jnp.exp(last_e)
    eacs = jnp.exp(acs_e)
    lane = lax.broadcasted_iota(jnp.int32, (q, LANES), 1)
    first_half = lane < SSD_HEADDIM

    for g in range(SSD_GROUPS):
        gs = slice(g * gw, (g + 1) * gw)
        ns = slice(g * SSD_STATE, (g + 1) * SSD_STATE)
        cg = c_ref[:, ns]
        bgt = bt_ref[ns, :]
        cb = _dot(cg, bgt)
        h_in = state_ref[:, gs]
        y_off = _dot(cg, h_in.astype(BF16)) * eacs[:, gs]
        st = _dot(bgt, xd[:, gs])
        state_ref[:, gs] = h_in * chunk_decay[:, gs] + st
        for jj in range(hg // 2):
            j = g * (hg // 2) + jj
            ms = []
            for hh in (2 * j, 2 * j + 1):
                ci = direction * SSD_HEADS + hh
                seg = acs[:, ci:ci + 1] - acst[ci:ci + 1, :]
                lm = jnp.exp(jnp.where(keep, seg, NEG_BIG))
                ms.append((cb * lm).astype(BF16))
            lhs = jnp.concatenate(ms, axis=1)
            x2 = xcb[:, j * LANES:(j + 1) * LANES]
            zero = jnp.zeros_like(x2)
            rhs = jnp.concatenate([jnp.where(first_half, x2, zero),
                                   jnp.where(first_half, zero, x2)], axis=0)
            y = _dot(lhs, rhs) + y_off[:, jj * LANES:(jj + 1) * LANES]
            if direction == 0:
                cs = slice(j * LANES, (j + 1) * LANES)
                y = y + xs[:, cs] * dskip_ref[:, cs]
            y_ref[:, j * LANES:(j + 1) * LANES] = y


def _ssd_kernel(xs_f, b_f, bt_f, c_f, dt_f, dtt_f, xs_b, b_b, bt_b, c_b, dt_b, dtt_b,
                bias_ref, biast_ref, alog_ref, alogt_ref, dskip_ref, rep_ref,
                yf_ref, yb_ref, sf_ref, sb_ref):
    @pl.when(pl.program_id(1) == 0)
    def _():
        sf_ref[...] = jnp.zeros_like(sf_ref)
        sb_ref[...] = jnp.zeros_like(sb_ref)

    _ssd_direction(0, xs_f, b_f, bt_f, c_f, dt_f, dtt_f, bias_ref, biast_ref, alog_ref, alogt_ref,
                   dskip_ref, rep_ref.at[0], yf_ref, sf_ref)
    _ssd_direction(1, xs_b, b_b, bt_b, c_b, dt_b, dtt_b, bias_ref, biast_ref, alog_ref, alogt_ref,
                   dskip_ref, rep_ref.at[1], yb_ref, sb_ref)


def _ssd_scan(xs, bm, bmt, cm, dt, dtt, dt_bias, a_log, d_skip):
    b, s, _ = xs.shape
    q = SSD_CHUNK
    nc = s // q
    nh2 = 2 * SSD_HEADS

    def fwd(bi, c):
        return c

    def bwd(bi, c):
        return nc - 1 - c

    def chunk_specs(cidx):
        return [pl.BlockSpec((None, q, D_INNER), lambda bi, c: (bi, cidx(bi, c), 0)),
                pl.BlockSpec((None, q, GN), lambda bi, c: (bi, cidx(bi, c), 0)),
                pl.BlockSpec((None, GN, q), lambda bi, c: (bi, 0, cidx(bi, c))),
                pl.BlockSpec((None, q, GN), lambda bi, c: (bi, cidx(bi, c), 0)),
                pl.BlockSpec((None, q, nh2), lambda bi, c: (bi, cidx(bi, c), 0)),
                pl.BlockSpec((nh2, q), lambda bi, c: (0, bi * nc + cidx(bi, c)))]

    def const(shape):
        return pl.BlockSpec(shape, lambda bi, c: (0,) * len(shape))

    rep = np.zeros((2, 4 * nh2, D_INNER), np.float32)
    for d in range(2):
        for h in range(SSD_HEADS):
            for piece in range(3):
                rep[d, piece * nh2 + d * SSD_HEADS + h, h * SSD_HEADDIM:(h + 1) * SSD_HEADDIM] = 1.0
    rep = jnp.asarray(rep, BF16)
    dskip_e = jnp.repeat(d_skip.astype(F32), SSD_HEADDIM).reshape(1, D_INNER)
    bias = dt_bias.reshape(1, nh2).astype(F32)
    alog = a_log.reshape(1, nh2).astype(F32)

    ins = (xs, bm, bmt, cm, dt.reshape(b, s, nh2), dtt)
    return pl.pallas_call(
        _ssd_kernel,
        grid=(b, nc),
        in_specs=chunk_specs(fwd) + chunk_specs(bwd) + [
            const((1, nh2)), const((nh2, 1)), const((1, nh2)), const((nh2, 1)),
            const((1, D_INNER)), const((2, 4 * nh2, D_INNER))],
        out_specs=[pl.BlockSpec((None, q, D_INNER), lambda bi, c: (bi, c, 0)),
                   pl.BlockSpec((None, q, D_INNER), lambda bi, c: (bi, nc - 1 - c, 0))],
        out_shape=[jax.ShapeDtypeStruct((b, s, D_INNER), F32),
                   jax.ShapeDtypeStruct((b, s, D_INNER), F32)],
        scratch_shapes=[pltpu.VMEM((SSD_STATE, D_INNER), F32),
                        pltpu.VMEM((SSD_STATE, D_INNER), F32)],
        compiler_params=_cparams("parallel", "arbitrary"),
        name="ssd_scan",
    )(*ins, *ins, bias, bias.reshape(nh2, 1), alog, alog.reshape(nh2, 1), dskip_e, rep)


def _flash_kernel(q_ref, k_ref, v_ref, o_ref, qa_ref, ka_ref, va_ref, kmax_ref, acc_ref, m_ref, l_ref,
                  *, bq, bk, unroll, bk_slow):
    seq = k_ref.shape[0]
    d = HEAD_DIM
    rows = GQA_REP * bq
    ones_sq = jnp.ones((d, d), BF16)
    first_lane = lax.broadcasted_iota(jnp.int32, (1, d), 1) == 0

    @pl.when(pl.program_id(2) == 0)
    def _():
        kmax_ref[...] = jnp.zeros_like(kmax_ref)
        one_col = jnp.where(first_lane, 1.0, 0.0).astype(BF16)

        def fill(c, carry):
            off = pl.multiple_of(c * bk_slow, bk_slow)
            kb = k_ref[pl.ds(off, bk_slow), :]
            ka_ref[pl.ds(off, bk_slow), 0:d] = kb
            ka_ref[pl.ds(off, bk_slow), d:2 * d] = jnp.broadcast_to(one_col, (bk_slow, d))
            va_ref[pl.ds(off, bk_slow), 0:d] = v_ref[pl.ds(off, bk_slow), :]
            va_ref[pl.ds(off, bk_slow), d:2 * d] = jnp.broadcast_to(one_col, (bk_slow, d))
            kf = kb.astype(F32)
            kn2 = _dot((kf * kf).astype(BF16), ones_sq)
            kmax_ref[...] = jnp.maximum(kmax_ref[...], jnp.max(kn2, axis=0, keepdims=True))
            return carry

        lax.fori_loop(0, seq // bk_slow, fill, 0)

    for r in range(GQA_REP):
        qa_ref[r * bq:(r + 1) * bq, 0:d] = q_ref[:, r * d:(r + 1) * d]
    qf = qa_ref[:, 0:d].astype(F32)
    qn2 = _dot((qf * qf).astype(BF16), ones_sq)
    shift = jnp.sqrt(qn2 * kmax_ref[...]) * NORM_SLACK
    qa_ref[:, d:2 * d] = jnp.where(first_lane, -shift, 0.0).astype(BF16)
    use_fast = jnp.max(shift) * 2.0 <= FAST_SOFTMAX_LOG2_RANGE

    def finish(acc, inv):
        for r in range(GQA_REP):
            rs = slice(r * bq, (r + 1) * bq)
            o_ref[:, r * d:(r + 1) * d] = (acc[rs, :] * inv[rs, :]).astype(o_ref.dtype)

    @pl.when(use_fast)
    def _():
        acc_ref[...] = jnp.zeros_like(acc_ref)

        def body(j, carry):
            qa = qa_ref[...]
            acc = acc_ref[...]
            for u in range(unroll):
                off = pl.multiple_of((j * unroll + u) * bk, bk)
                s = _dot_nt(qa, ka_ref[pl.ds(off, bk), :])
                acc = acc + _dot(jnp.exp2(s).astype(BF16), va_ref[pl.ds(off, bk), :])
            acc_ref[...] = acc
            return carry

        lax.fori_loop(0, seq // (bk * unroll), body, 0)
        acc = acc_ref[...]
        finish(acc[:, 0:d], 1.0 / acc[:, d:d + 1])

    @pl.when(jnp.logical_not(use_fast))
    def _():
        m_ref[...] = jnp.full_like(m_ref, NEG_BIG)
        l_ref[...] = jnp.zeros_like(l_ref)
        acc_ref[...] = jnp.zeros_like(acc_ref)

        def body(j, carry):
            off = pl.multiple_of(j * bk_slow, bk_slow)
            s = _dot_nt(qa_ref[:, 0:d], k_ref[pl.ds(off, bk_slow), :])
            m_prev = m_ref[...]
            m_new = jnp.maximum(m_prev, jnp.max(s, axis=-1, keepdims=True))
            alpha = jnp.exp2(m_prev - m_new)
            p = jnp.exp2(s - m_new)
            l_ref[...] = alpha * l_ref[...] + jnp.sum(p, axis=-1, keepdims=True)
            acc_ref[:, 0:d] = alpha * acc_ref[:, 0:d] + _dot(p.astype(BF16), v_ref[pl.ds(off, bk_slow), :])
            m_ref[...] = m_new
            return carry

        lax.fori_loop(0, seq // bk_slow, body, 0)
        finish(acc_ref[:, 0:d], 1.0 / l_ref[...])


def _flash_gqa(q, k, v, bq=256, bk=512, unroll=16, bk_slow=1024):
    b, s, _ = q.shape
    bk = min(bk, s)
    bk_slow = min(bk_slow, s)
    unroll = min(unroll, s // bk)
    rows = GQA_REP * bq
    gw = GQA_REP * HEAD_DIM
    return pl.pallas_call(
        functools.partial(_flash_kernel, bq=bq, bk=bk, unroll=unroll, bk_slow=bk_slow),
        grid=(b, GQA_KV_HEADS, s // bq),
        in_specs=[pl.BlockSpec((None, bq, gw), lambda bi, kh, i: (bi, i, kh)),
                  pl.BlockSpec((None, s, HEAD_DIM), lambda bi, kh, i: (bi, 0, kh)),
                  pl.BlockSpec((None, s, HEAD_DIM), lambda bi, kh, i: (bi, 0, kh))],
        out_specs=pl.BlockSpec((None, bq, gw), lambda bi, kh, i: (bi, i, kh)),
        out_shape=jax.ShapeDtypeStruct((b, s, GQA_WIDTH), BF16),
        scratch_shapes=[pltpu.VMEM((rows, 2 * HEAD_DIM), BF16),
                        pltpu.VMEM((s, 2 * HEAD_DIM), BF16),
                        pltpu.VMEM((s, 2 * HEAD_DIM), BF16),
                        pltpu.VMEM((1, HEAD_DIM), F32),
                        pltpu.VMEM((rows, 2 * HEAD_DIM), F32),
                        pltpu.VMEM((rows, 1), F32),
                        pltpu.VMEM((rows, 1), F32)],
        compiler_params=_cparams("parallel", "parallel", "arbitrary"),
        name="flash_gqa",
    )(q, k, v)


def _band_kernel(q_ref, kp_ref, kc_ref, kn_ref, vp_ref, vc_ref, vn_ref, bias_ref, o_ref, lse_ref,
                 *, dil, nsub, length):
    i = pl.program_id(1)
    tq = BAND_TILE
    col = lax.broadcasted_iota(jnp.int32, (tq, 2 * tq), 1)
    for u in range(nsub):
        kpos = (i * nsub + u) * tq - DIL_HALF + col
        inside = (kpos >= 0) & (kpos < length)
        rows = slice(u * tq, (u + 1) * tq)
        head = slice(u * tq - DIL_HALF, u * tq)
        tail = slice((u + 1) * tq, (u + 1) * tq + DIL_HALF)
        for r in range(dil):
            for hd in range(DIL_HEADS_PER_GROUP):
                c0 = (r * DIL_HEADS_PER_GROUP + hd) * HEAD_DIM
                sl = slice(c0, c0 + HEAD_DIM)

                def window(prev_ref, cur_ref, next_ref):
                    first = prev_ref[:, sl] if u == 0 else cur_ref[head, sl]
                    last = next_ref[:, sl] if u == nsub - 1 else cur_ref[tail, sl]
                    return jnp.concatenate([first, cur_ref[rows, sl], last], axis=0)

                s = _dot_nt(q_ref[rows, sl], window(kp_ref, kc_ref, kn_ref)) + bias_ref[hd]
                s = jnp.where(inside, s, NEG_BIG)
                m = jnp.max(s, axis=-1, keepdims=True)
                p = jnp.exp(s - m)
                l = jnp.sum(p, axis=-1, keepdims=True)
                o_ref[rows, sl] = _dot(p.astype(BF16), window(vp_ref, vc_ref, vn_ref)) / l
                lse_ref[rows, sl] = jnp.broadcast_to(m + jnp.log(l), (tq, HEAD_DIM))


def _band_attention(qr, kr, vr, bias_t, group, dil, batch):
    rows_total, width = qr.shape
    length = rows_total // batch
    nsub = max(1, BAND_TILES_PER_STEP // dil)
    bqr = nsub * BAND_TILE
    nb = length // bqr
    hb = bqr // DIL_HALF
    last_hb = length // DIL_HALF - 1
    view = lambda t: t.reshape(batch, length, width)

    cur = pl.BlockSpec((None, bqr, width), lambda bi, i: (bi, i, 0))
    prev = pl.BlockSpec((None, DIL_HALF, width), lambda bi, i: (bi, jnp.maximum(i * hb - 1, 0), 0))
    nxt = pl.BlockSpec((None, DIL_HALF, width), lambda bi, i: (bi, jnp.minimum((i + 1) * hb, last_hb), 0))
    o, lse = pl.pallas_call(
        functools.partial(_band_kernel, dil=dil, nsub=nsub, length=length),
        grid=(batch, nb),
        in_specs=[cur, prev, cur, nxt, prev, cur, nxt,
                  pl.BlockSpec((DIL_HEADS_PER_GROUP, BAND_TILE, 2 * BAND_TILE), lambda bi, i: (0, 0, 0))],
        out_specs=[cur, cur],
        out_shape=[jax.ShapeDtypeStruct((batch, length, width), F32),
                   jax.ShapeDtypeStruct((batch, length, width), F32)],
        compiler_params=_cparams("parallel", "parallel"),
        name=f"band_attn_g{group}",
    )(view(qr), view(kr), view(kr), view(kr), view(vr), view(vr), view(vr), bias_t)
    return o.reshape(rows_total, width), lse.reshape(rows_total, width)


def _t5_bucket(rel):
    nb = N_BUCKETS // 2
    max_exact = nb // 2
    ret = jnp.where(rel > 0, nb, 0)
    n = jnp.abs(rel)
    nf = jnp.maximum(n, 1).astype(F32)
    large = max_exact + (jnp.log(nf / max_exact) / math.log(REL_MAX_DIST / max_exact)
                         * (nb - max_exact)).astype(jnp.int32)
    large = jnp.minimum(large, nb - 1)
    return ret + jnp.where(n < max_exact, n, large)


def _band_bias_tiles(rel_bias):
    i = np.arange(BAND_TILE)[:, None]
    j = np.arange(2 * BAND_TILE)[None, :]
    rel = (j - DIL_HALF) - i
    band = np.abs(rel) <= DIL_HALF
    idx = np.clip(rel + DIL_HALF, 0, 2 * DIL_HALF)
    tiles = []
    for g, (_, dil) in enumerate(DIL_PATTERNS):
        dist = jnp.arange(-DIL_HALF, DIL_HALF + 1, dtype=jnp.int32) * dil
        tbl = rel_bias.astype(F32)[_t5_bucket(dist)]
        tbl = tbl[:, g * DIL_HEADS_PER_GROUP:(g + 1) * DIL_HEADS_PER_GROUP].T
        tiles.append(jnp.where(band[None], tbl[:, idx], NEG_BIG))
    return tiles


def _to_token_order(src_ref, scr_ref, dil):
    if dil == 1:
        return src_ref[...]
    n = src_ref.shape[0]
    nlt = DIL_OUT // LANES
    for r in range(dil):
        for c in range(nlt):
            c0 = r * DIL_OUT + c * LANES
            scr_ref[c, pl.ds(r, n, stride=dil), :] = src_ref[:, c0:c0 + LANES]
    return jnp.concatenate([scr_ref[c] for c in range(nlt)], axis=1)


def _merge_kernel(x_ref, yf_ref, yb_ref, zs_ref, gssd_ref, ygqa_ref,
                  o0_ref, l0_ref, o1_ref, l1_ref, o2_ref, l2_ref, gates_ref,
                  wssd_ref, wgqa_ref, wdil_ref, wout_ref, gpost_ref, o_ref, *scr):
    y = (yf_ref[...] + yb_ref[...]) * zs_ref[...]
    y_ssd = _rms(y, gssd_ref[...]).astype(BF16)
    dils = [dil for _, dil in DIL_PATTERNS]
    o0, o1, o2 = (_to_token_order(r, s, dl) for r, s, dl in zip((o0_ref, o1_ref, o2_ref), scr[0:3], dils))
    l0, l1, l2 = (_to_token_order(r, s, dl) for r, s, dl in zip((l0_ref, l1_ref, l2_ref), scr[3:6], dils))
    mx = jnp.maximum(jnp.maximum(l0, l1), l2)
    e0, e1, e2 = jnp.exp(l0 - mx), jnp.exp(l1 - mx), jnp.exp(l2 - mx)
    y_dil = ((e0 * o0 + e1 * o1 + e2 * o2) / (e0 + e1 + e2)).astype(BF16)
    d = D_MODEL
    mix = gates_ref[:, 0:d] * _dot(y_ssd, wssd_ref[...])
    mix = mix + gates_ref[:, d:2 * d] * _dot(ygqa_ref[...], wgqa_ref[...])
    mix = mix + gates_ref[:, 2 * d:3 * d] * _dot(y_dil, wdil_ref[...])
    out = _dot(mix.astype(BF16), wout_ref[...])
    o_ref[...] = x_ref[...] + _rms(out, gpost_ref[...])


def _merge(x2d, yf, yb, zs, g_ssd, y_gqa, dil_parts, gates, w_ssd, w_gqa, w_dil, w_out, g_post, bm=256):
    t, d = x2d.shape
    row = lambda n: pl.BlockSpec((bm, n), lambda i: (i, 0))
    const = lambda a: pl.BlockSpec(a.shape, lambda i: (0, 0))
    (o0, l0), (o1, l1), (o2, l2) = dil_parts
    g_ssd = g_ssd.reshape(1, -1)
    g_post = g_post.reshape(1, -1)
    args = (x2d, yf, yb, zs, g_ssd, y_gqa, o0, l0, o1, l1, o2, l2, gates, w_ssd, w_gqa, w_dil, w_out, g_post)
    specs = [row(d), row(D_INNER), row(D_INNER), row(D_INNER), const(g_ssd), row(GQA_WIDTH)]
    for _, dil in DIL_PATTERNS:
        specs += [pl.BlockSpec((bm // dil, dil * DIL_OUT), lambda i: (i, 0))] * 2
    specs += [row(N_BRANCHES * d), const(w_ssd), const(w_gqa), const(w_dil), const(w_out), const(g_post)]
    return pl.pallas_call(
        _merge_kernel,
        grid=(t // bm,),
        in_specs=specs,
        out_specs=row(d),
        out_shape=jax.ShapeDtypeStruct((t, d), F32),
        scratch_shapes=[pltpu.VMEM((DIL_OUT // LANES, bm, LANES), F32)] * 6,
        compiler_params=_cparams("parallel"),
        name="merge_out",
    )(*args)


def _ffn_kernel(x_ref, p_ref, gpre_ref, wg_ref, wu_ref, wd_ref, gpost_ref, gple_ref, wpg_ref, wp_ref,
                o_ref, *, nc):
    x = x_ref[...]
    h = _rms(x, gpre_ref[...]).astype(BF16)
    ff = jnp.zeros(x.shape, F32)
    for c in range(0, wg_ref.shape[1], nc):
        a = _silu(_dot(h, wg_ref[:, c:c + nc])) * _dot(h, wu_ref[:, c:c + nc])
        ff = ff + _dot(a.astype(BF16), wd_ref[c:c + nc, :])
    x = x + _rms(ff, gpost_ref[...])
    hn = _rms(x, gple_ref[...]).astype(BF16)
    gate = _sigmoid(_dot(hn, wpg_ref[...]))
    o_ref[...] = x + _dot(p_ref[...].astype(BF16), wp_ref[...]) * gate


def _ffn_ple(x2d, p2d, g_pre, w_gate, w_up, w_down, g_post, g_ple, w_ple_gate, w_ple, bm=256):
    t, d = x2d.shape
    row = lambda n: pl.BlockSpec((bm, n), lambda i: (i, 0))
    const = lambda a: pl.BlockSpec(a.shape, lambda i: (0, 0))
    g_pre, g_post, g_ple = (g.reshape(1, d) for g in (g_pre, g_post, g_ple))
    args = (x2d, p2d, g_pre, w_gate, w_up, w_down, g_post, g_ple, w_ple_gate, w_ple)
    specs = [row(d), row(PLE_DIM)] + [const(a) for a in args[2:]]
    nc = 256 if FFN_DIM % 256 == 0 else FFN_DIM
    return pl.pallas_call(
        functools.partial(_ffn_kernel, nc=nc),
        grid=(t // bm,),
        in_specs=specs,
        out_specs=row(d),
        out_shape=jax.ShapeDtypeStruct((t, d), F32),
        compiler_params=_cparams("parallel"),
        name="ffn_ple",
    )(*args)


def _rope_tables(seq):
    pos = jnp.arange(seq)
    row = (pos // GRID_W).astype(F32)
    colp = (pos % GRID_W).astype(F32)
    n_pairs = HEAD_DIM // 4
    inv = ROPE_THETA ** (-jnp.arange(n_pairs, dtype=F32) / n_pairs)
    ang = jnp.concatenate([row[:, None] * inv, colp[:, None] * inv], axis=-1)
    c, s = jnp.cos(ang), jnp.sin(ang)
    return jnp.concatenate([c, c], axis=-1), jnp.concatenate([-s, s], axis=-1)


def _layer(x, p_i, i, P, cosf, sinf, bias_tiles):
    b, s, d = x.shape
    t = b * s
    x2d = x.reshape(t, d)
    offs = np.concatenate([[0], np.cumsum(IN_WIDTHS)])
    w_in = P["w_in"][i]
    wz, wxbc, wdt, wq, wk, wv, wdq, wdk, wdv, wgates = (
        w_in[:, offs[j]:offs[j + 1]] for j in range(len(IN_WIDTHS)))
    def deinterleave(w):
        lead = w.shape[:-1]
        wh = w.reshape(*lead, -1, HEAD_DIM)
        return jnp.concatenate([wh[..., 0::2], wh[..., 1::2]], axis=-1).reshape(*lead, -1)

    wq, wk = deinterleave(wq), deinterleave(wk)
    bf = lambda w: w.astype(BF16)

    h = _prenorm(x2d, P["g_pre_mix"][i])
    zs = _proj(h, bf(wz), BF16, act="silu", name="proj_z")
    xs, bm, bmt, cm = _proj_conv(h, bf(wxbc), P["conv_w"][i], P["conv_b"][i], b)
    dt, dtt = _proj_dt(h, bf(wdt))
    q = _proj_qk(h, bf(wq), deinterleave(P["g_q"][i]), cosf, sinf, ATTN_SCALE * LOG2E, s, name="proj_q")
    k = _proj_qk(h, bf(wk), deinterleave(P["g_k"][i]), cosf, sinf, 1.0, s, name="proj_k")
    v = _proj(h, bf(wv), BF16, name="proj_v")
    dil_qkv = _proj_dil(h, bf(jnp.concatenate([wdq, wdk, wdv], axis=1)))
    gates = _proj(h, bf(wgates), BF16, act="sigmoid", name="proj_gates")

    yf, yb = _ssd_scan(xs.reshape(b, s, D_INNER), bm.reshape(b, s, GN), bmt, cm.reshape(b, s, GN), dt, dtt,
                       P["dt_bias"][i], P["a_log"][i], P["d_skip"][i])

    y_gqa = _flash_gqa(q.reshape(b, s, GQA_WIDTH), k.reshape(b, s, GQA_KV_WIDTH),
                       v.reshape(b, s, GQA_KV_WIDTH))

    dil_parts = []
    ng = len(DIL_PATTERNS)
    for g, (_, dil) in enumerate(DIL_PATTERNS):
        dil_parts.append(_band_attention(dil_qkv[g], dil_qkv[ng + g], dil_qkv[2 * ng + g],
                                         bias_tiles[g], g, dil, b))

    x2d = _merge(x2d, yf.reshape(t, D_INNER), yb.reshape(t, D_INNER), zs, P["g_ssd"][i],
                 y_gqa.reshape(t, GQA_WIDTH), dil_parts, gates,
                 bf(P["w_br_ssd"][i]), bf(P["w_br_gqa"][i]), bf(P["w_br_dil"][i]), bf(P["w_out"][i]),
                 P["g_post_mix"][i])
    x2d = _ffn_ple(x2d, p_i.reshape(t, PLE_DIM), P["g_pre_ffn"][i], bf(P["w_gate"][i]), bf(P["w_up"][i]),
                   bf(P["w_down"][i]), P["g_post_ffn"][i], P["g_ple"][i], bf(P["w_ple_gate"][i]),
                   bf(P["w_ple"][i]))
    return x2d.reshape(b, s, d)


def _trunk(x, p, P, rel_bias):
    s = x.shape[1]
    cosf, sinf = _rope_tables(s)
    bias_tiles = _band_bias_tiles(rel_bias)
    for i in range(p.shape[0]):
        x = _layer(x, p[i], i, P, cosf, sinf, bias_tiles)
    return x


def kernel(x_prompt, x_sample, p_prompt, p_sample, w_in, conv_w, conv_b, dt_bias, a_log, d_skip, g_ssd, g_q, g_k, w_br_ssd, w_br_gqa, w_br_dil, w_out, g_pre_mix, g_post_mix, g_pre_ffn, g_post_ffn, w_gate, w_up, w_down, w_ple, g_ple, w_ple_gate, rel_bias):
    P = dict(w_in=w_in, conv_w=conv_w, conv_b=conv_b, dt_bias=dt_bias, a_log=a_log, d_skip=d_skip,
             g_ssd=g_ssd, g_q=g_q, g_k=g_k, w_br_ssd=w_br_ssd, w_br_gqa=w_br_gqa, w_br_dil=w_br_dil,
             w_out=w_out, g_pre_mix=g_pre_mix, g_post_mix=g_post_mix, g_pre_ffn=g_pre_ffn,
             g_post_ffn=g_post_ffn, w_gate=w_gate, w_up=w_up, w_down=w_down, w_ple=w_ple,
             g_ple=g_ple, w_ple_gate=w_ple_gate)
    nb = x_prompt.shape[0]
    if x_prompt.shape[1:] == x_sample.shape[1:]:
        x = jnp.concatenate([x_prompt, x_sample], axis=0)
        p = jnp.concatenate([p_prompt, p_sample], axis=1)
        y = _trunk(x, p, P, rel_bias)
        return (y[:nb], y[nb:])
    return (_trunk(x_prompt, p_prompt, P, rel_bias), _trunk(x_sample, p_sample, P, rel_bias))
```

```python
import functools
import math

import jax
import jax.numpy as jnp
import numpy as np
from jax import lax
from jax.experimental import pallas as pl
from jax.experimental.pallas import tpu as pltpu

F32 = jnp.float32
BF16 = jnp.bfloat16

D_MODEL = 1024
GRID_W = 64
PLE_DIM = 256
EPS = 1e-6
SSD_HEADS = 32
SSD_HEADDIM = 64
D_INNER = SSD_HEADS * SSD_HEADDIM
SSD_GROUPS = 4
SSD_STATE = 128
SSD_CHUNK = 128
CONV_W = 5
GN = SSD_GROUPS * SSD_STATE
XBC_DIM = D_INNER + 2 * GN
HEAD_DIM = 128
GQA_Q_HEADS = 16
GQA_KV_HEADS = 4
GQA_REP = GQA_Q_HEADS // GQA_KV_HEADS
GQA_WIDTH = GQA_Q_HEADS * HEAD_DIM
GQA_KV_WIDTH = GQA_KV_HEADS * HEAD_DIM
ROPE_THETA = 10000.0
ATTN_SCALE = HEAD_DIM ** -0.5
DIL_PATTERNS = ((128, 1), (512, 4), (2048, 16))
DIL_HEADS_PER_GROUP = 4
DIL_HEADS = len(DIL_PATTERNS) * DIL_HEADS_PER_GROUP
DIL_WIDTH = DIL_HEADS * HEAD_DIM
DIL_OUT = DIL_HEADS_PER_GROUP * HEAD_DIM
DIL_HALF = 64
BAND_TILE = 2 * DIL_HALF
BAND_TILES_PER_STEP = 16
N_BUCKETS = 32
REL_MAX_DIST = 2048
FFN_DIM = ((8 * D_MODEL // 3 + 255) // 256) * 256
N_BRANCHES = 3
IN_WIDTHS = (D_INNER, XBC_DIM, 2 * SSD_HEADS, GQA_WIDTH, GQA_KV_WIDTH, GQA_KV_WIDTH,
             DIL_WIDTH, DIL_WIDTH, DIL_WIDTH, N_BRANCHES * D_MODEL)

LANES = 128
SUBLANES = 8
BF16_ROWS = 16
MXU_COLS = 256
VMEM_LIMIT = 56 * 1024 * 1024
NEG_BIG = -1e30
LOG2E = math.log2(math.e)
VT_ROWS = HEAD_DIM + BF16_ROWS
NORM_SLACK = 1.02
FAST_SOFTMAX_LOG2_RANGE = 100.0


def _cparams(*sem):
    return pltpu.CompilerParams(dimension_semantics=sem, vmem_limit_bytes=VMEM_LIMIT)


def _rms(x, g):
    ms = jnp.mean(x * x, axis=-1, keepdims=True)
    return x * lax.rsqrt(ms + EPS) * g


def _sigmoid(x):
    return 1.0 / (1.0 + jnp.exp(-x))


def _silu(x):
    return x * _sigmoid(x)


def _softplus(x):
    return jnp.maximum(x, 0.0) + jnp.log1p(jnp.exp(-jnp.abs(x)))


def _dot(a, b):
    return jnp.dot(a, b, preferred_element_type=F32)


def _dot_nt(a, b):
    return lax.dot_general(a, b, (((1,), (1,)), ((), ())), preferred_element_type=F32)


def _split3(x):
    hi = x.astype(BF16)
    r1 = x - hi.astype(F32)
    mid = r1.astype(BF16)
    lo = (r1 - mid.astype(F32)).astype(BF16)
    return hi, mid, lo


def _dot_exact_lhs(x, m_bf16, pieces=3):
    parts = _split3(x)[:pieces]
    acc = _dot(parts[0], m_bf16)
    for p in parts[1:]:
        acc = acc + _dot(p, m_bf16)
    return acc


def _dot_exact_rhs(m_bf16, x, pieces=3):
    parts = _split3(x)[:pieces]
    acc = _dot(m_bf16, parts[0])
    for p in parts[1:]:
        acc = acc + _dot(m_bf16, p)
    return acc


def _prenorm_cat_kernel(xa_ref, xb_ref, g_ref, h_ref, x_ref, *, na):
    def emit(src_ref):
        x = src_ref[...]
        h_ref[...] = _rms(x, g_ref[...]).astype(h_ref.dtype)
        x_ref[...] = x

    pl.when(pl.program_id(0) < na)(lambda: emit(xa_ref))
    pl.when(pl.program_id(0) >= na)(lambda: emit(xb_ref))


def _prenorm_cat(xa, xb, g, bm=1024):
    (ta, d), tb = xa.shape, xb.shape[0]
    na = ta // bm
    t = ta + tb
    return pl.pallas_call(
        functools.partial(_prenorm_cat_kernel, na=na),
        grid=(t // bm,),
        in_specs=[pl.BlockSpec((bm, d), lambda i: (jnp.minimum(i, na - 1), 0)),
                  pl.BlockSpec((bm, d), lambda i: (jnp.maximum(i - na, 0), 0)),
                  pl.BlockSpec((1, d), lambda i: (0, 0))],
        out_specs=[pl.BlockSpec((bm, d), lambda i: (i, 0)),
                   pl.BlockSpec((bm, d), lambda i: (i, 0))],
        out_shape=[jax.ShapeDtypeStruct((t, d), BF16),
                   jax.ShapeDtypeStruct((t, d), F32)],
        compiler_params=_cparams("arbitrary"),
        name="prenorm_cat",
    )(xa, xb, g.reshape(1, d))


def _proj_kernel(h_ref, w_ref, o_ref, *, act, nc):
    h = h_ref[...]
    n = o_ref.shape[-1]
    for c in range(0, n, nc):
        y = _dot(h, w_ref[:, c:c + nc])
        if act == "silu":
            y = _silu(y)
        elif act == "sigmoid":
            y = _sigmoid(y)
        o_ref[:, c:c + nc] = y.astype(o_ref.dtype)


def _proj(h, w, out_dtype, act=None, bm=512, name="proj"):
    t, d = h.shape
    n = w.shape[1]
    nc = 512 if n % 512 == 0 else n
    return pl.pallas_call(
        functools.partial(_proj_kernel, act=act, nc=nc),
        grid=(t // bm,),
        in_specs=[pl.BlockSpec((bm, d), lambda i: (i, 0)),
                  pl.BlockSpec((d, n), lambda i: (0, 0))],
        out_specs=pl.BlockSpec((bm, n), lambda i: (i, 0)),
        out_shape=jax.ShapeDtypeStruct((t, n), out_dtype),
        compiler_params=_cparams("parallel"),
        name=name,
    )(h, w)


def _proj_qk_kernel(h_ref, w_ref, g_ref, cos_ref, sin_ref, o_ref, *, scale):
    h = h_ref[...]
    g = g_ref[...]
    cosf = cos_ref[...]
    sinf = sin_ref[...]
    for hd in range(o_ref.shape[-1] // HEAD_DIM):
        sl = slice(hd * HEAD_DIM, (hd + 1) * HEAD_DIM)
        y = _rms(_dot(h, w_ref[:, sl]), g)
        y = y * cosf + pltpu.roll(y, HEAD_DIM // 2, 1) * sinf
        if scale != 1.0:
            y = y * scale
        o_ref[:, sl] = y.astype(o_ref.dtype)


def _proj_qk(h, w, g, cosf, sinf, scale, seq, bm=512, name="proj_qk"):
    t, d = h.shape
    n = w.shape[1]
    nsb = seq // bm
    return pl.pallas_call(
        functools.partial(_proj_qk_kernel, scale=scale),
        grid=(t // bm,),
        in_specs=[pl.BlockSpec((bm, d), lambda i: (i, 0)),
                  pl.BlockSpec((d, n), lambda i: (0, 0)),
                  pl.BlockSpec((1, HEAD_DIM), lambda i: (0, 0)),
                  pl.BlockSpec((bm, HEAD_DIM), lambda i: (i % nsb, 0)),
                  pl.BlockSpec((bm, HEAD_DIM), lambda i: (i % nsb, 0))],
        out_specs=pl.BlockSpec((bm, n), lambda i: (i, 0)),
        out_shape=jax.ShapeDtypeStruct((t, n), BF16),
        compiler_params=_cparams("parallel"),
        name=name,
    )(h, w, g.reshape(1, HEAD_DIM), cosf, sinf)


def _proj_dil_kernel(h_ref, w_ref, *refs):
    outs, scr = refs[:-1], refs[-1]
    h = h_ref[...]
    bm = h.shape[0]
    nlt = DIL_OUT // LANES
    for t in range(3):
        for g, (_, dil) in enumerate(DIL_PATTERNS):
            o_ref = outs[t * len(DIL_PATTERNS) + g]
            c0 = (t * len(DIL_PATTERNS) + g) * DIL_OUT
            y = _dot(h, w_ref[:, c0:c0 + DIL_OUT])
            if t == 0:
                y = y * ATTN_SCALE
            if dil == 1:
                o_ref[...] = y.astype(o_ref.dtype)
                continue
            for c in range(nlt):
                scr[c] = y[:, c * LANES:(c + 1) * LANES]
            for r in range(dil):
                for c in range(nlt):
                    o0 = r * DIL_OUT + c * LANES
                    o_ref[:, o0:o0 + LANES] = scr[c, pl.ds(r, bm // dil, stride=dil), :].astype(o_ref.dtype)


def _proj_dil(h, w, bm=512):
    t, d = h.shape
    n = w.shape[1]
    out_specs, out_shapes = [], []
    for _ in range(3):
        for _, dil in DIL_PATTERNS:
            out_specs.append(pl.BlockSpec((bm // dil, dil * DIL_OUT), lambda i: (i, 0)))
            out_shapes.append(jax.ShapeDtypeStruct((t // dil, dil * DIL_OUT), BF16))
    return pl.pallas_call(
        _proj_dil_kernel,
        grid=(t // bm,),
        in_specs=[pl.BlockSpec((bm, d), lambda i: (i, 0)),
                  pl.BlockSpec((d, n), lambda i: (0, 0))],
        out_specs=out_specs,
        out_shape=out_shapes,
        scratch_shapes=[pltpu.VMEM((DIL_OUT // LANES, bm, LANES), F32)],
        compiler_params=_cparams("parallel"),
        name="proj_dil",
    )(h, w)


def _proj_dt_kernel(h_ref, w_ref, wt_ref, o_ref, ot_ref):
    h = h_ref[...]
    o_ref[...] = _dot(h, w_ref[...])
    ot_ref[...] = _dot_nt(wt_ref[...], h)


def _proj_dt(h, w, bm=512):
    t, d = h.shape
    n = w.shape[1]
    return pl.pallas_call(
        _proj_dt_kernel,
        grid=(t // bm,),
        in_specs=[pl.BlockSpec((bm, d), lambda i: (i, 0)),
                  pl.BlockSpec((d, n), lambda i: (0, 0)),
                  pl.BlockSpec((n, d), lambda i: (0, 0))],
        out_specs=[pl.BlockSpec((bm, n), lambda i: (i, 0)),
                   pl.BlockSpec((n, bm), lambda i: (0, i))],
        out_shape=[jax.ShapeDtypeStruct((t, n), F32),
                   jax.ShapeDtypeStruct((n, t), F32)],
        compiler_params=_cparams("parallel"),
        name="proj_dt",
    )(h, w, w.T)


def _proj_conv_kernel(h_ref, hp_ref, hn_ref, w_ref, cw_ref, cb_ref, xs_ref, b_out, bt_out, c_out, *ext_refs,
                      nsb):
    i = pl.program_id(0)
    bm = h_ref.shape[0]
    n = w_ref.shape[1]
    halo = SUBLANES
    pad = CONV_W // 2
    keep_prev = (i % nsb > 0).astype(F32)
    keep_next = (i % nsb < nsb - 1).astype(F32)
    nc = ext_refs[0].shape[1]
    for ext_ref, c in zip(ext_refs, range(0, n, nc)):
        w = w_ref[:, c:c + nc]
        ext_ref[halo:halo + bm, :] = _dot(h_ref[...], w)
        ext_ref[0:halo, :] = _dot(hp_ref[...], w)[BF16_ROWS - halo:, :] * keep_prev
        ext_ref[halo + bm:halo + bm + halo, :] = _dot(hn_ref[...], w)[0:halo, :] * keep_next
    rc = LANES
    for c in range(0, n, LANES):
        cs = slice(c, c + LANES)
        ext_ref = ext_refs[c // nc]
        es = slice(c % nc, c % nc + LANES)
        taps = [cw_ref[k:k + 1, cs] for k in range(CONV_W)]
        bias = cb_ref[:, cs]
        for r in range(0, bm, rc):
            base = halo - pad + r
            acc = bias + ext_ref[base:base + rc, es] * taps[0]
            for k in range(1, CONV_W):
                acc = acc + ext_ref[base + k:base + k + rc, es] * taps[k]
            y = _silu(acc)
            rs = slice(r, r + rc)
            if c < D_INNER:
                xs_ref[rs, cs] = y
            elif c < D_INNER + GN:
                cc = slice(c - D_INNER, c - D_INNER + LANES)
                b_out[rs, cc] = y.astype(b_out.dtype)
                bt_out[cc, rs] = y.T.astype(bt_out.dtype)
            else:
                cc = slice(c - D_INNER - GN, c - D_INNER - GN + LANES)
                c_out[rs, cc] = y.astype(c_out.dtype)


def _proj_conv(h, w, conv_w, conv_b, batch, bm=512):
    t, d = h.shape
    n = w.shape[1]
    s = t // batch
    nsb = s // bm
    hb = bm // BF16_ROWS
    last_hb = t // BF16_ROWS - 1
    nc = 2 * MXU_COLS
    return pl.pallas_call(
        functools.partial(_proj_conv_kernel, nsb=nsb),
        grid=(t // bm,),
        in_specs=[pl.BlockSpec((bm, d), lambda i: (i, 0)),
                  pl.BlockSpec((BF16_ROWS, d), lambda i: (jnp.maximum(i * hb - 1, 0), 0)),
                  pl.BlockSpec((BF16_ROWS, d), lambda i: (jnp.minimum((i + 1) * hb, last_hb), 0)),
                  pl.BlockSpec((d, n), lambda i: (0, 0)),
                  pl.BlockSpec((CONV_W, n), lambda i: (0, 0)),
                  pl.BlockSpec((1, n), lambda i: (0, 0))],
        out_specs=[pl.BlockSpec((bm, D_INNER), lambda i: (i, 0)),
                   pl.BlockSpec((bm, GN), lambda i: (i, 0)),
                   pl.BlockSpec((None, GN, bm), lambda i: (i // nsb, 0, i % nsb)),
                   pl.BlockSpec((bm, GN), lambda i: (i, 0))],
        out_shape=[jax.ShapeDtypeStruct((t, D_INNER), F32),
                   jax.ShapeDtypeStruct((t, GN), BF16),
                   jax.ShapeDtypeStruct((batch, GN, s), BF16),
                   jax.ShapeDtypeStruct((t, GN), BF16)],
        scratch_shapes=[pltpu.VMEM((bm + 2 * SUBLANES, nc), F32)] * (n // nc),
        compiler_params=_cparams("parallel"),
        name="proj_conv",
    )(h, h, h, w, conv_w, conv_b.reshape(1, n))


def _ssd_direction(direction, xs_ref, b_ref, bt_ref, c_ref, dt_ref, dtt_ref, bias_ref, biast_ref,
                   alog_ref, alogt_ref, dskip_ref, rep_ref, y_ref, state_ref):
    q = SSD_CHUNK
    hg = SSD_HEADS // SSD_GROUPS
    gw = hg * SSD_HEADDIM
    row = lax.broadcasted_iota(jnp.int32, (q, q), 0)
    col = lax.broadcasted_iota(jnp.int32, (q, q), 1)
    if direction == 0:
        keep = col <= row
        last = q - 1
    else:
        keep = col >= row
        last = 0
    tri = keep.astype(BF16)
    tri_t = (row <= col).astype(BF16) if direction == 0 else (row >= col).astype(BF16)

    dt = _softplus(dt_ref[...] + bias_ref[...])
    dtt = _softplus(dtt_ref[...] + biast_ref[...])
    dta = dt * (-jnp.exp(alog_ref[...]))
    dtat = dtt * (-jnp.exp(alogt_ref[...]))
    acs = _dot_exact_rhs(tri, dta)
    acst = _dot_exact_lhs(dtat, tri_t)

    rep3 = rep_ref[...]

    def expand(x):
        wide = jnp.concatenate([x, x, x, x], axis=1)
        hi, mid, lo = _split3(wide)
        lane = lax.broadcasted_iota(jnp.int32, wide.shape, 1)
        packed = jnp.where(lane < x.shape[1], hi, jnp.where(lane < 2 * x.shape[1], mid, lo))
        return _dot(packed, rep3)

    acs_e = expand(acs)
    dt_e = expand(dt)
    xs = xs_ref[...]
    xc = xs * dt_e
    xcb = xc.astype(BF16)
    last_e = acs_e[last:last + 1, :]
    xd = (xc * jnp.exp(last_e - acs_e)).astype(BF16)
    chunk_decay = jnp.exp(last_e)
    eacs = jnp.exp(acs_e)
    lane = lax.broadcasted_iota(jnp.int32, (q, LANES), 1)
    first_half = lane < SSD_HEADDIM

    for g in range(SSD_GROUPS):
        gs = slice(g * gw, (g + 1) * gw)
        ns = slice(g * SSD_STATE, (g + 1) * SSD_STATE)
        cg = c_ref[:, ns]
        bgt = bt_ref[ns, :]
        cb = _dot(cg, bgt)
        h_in = state_ref[:, gs]
        y_off = _dot(cg, h_in.astype(BF16)) * eacs[:, gs]
        st = _dot(bgt, xd[:, gs])
        state_ref[:, gs] = h_in * chunk_decay[:, gs] + st
        for jj in range(hg // 2):
            j = g * (hg // 2) + jj
            ms = []
            for hh in (2 * j, 2 * j + 1):
                ci = direction * SSD_HEADS + hh
                seg = acs[:, ci:ci + 1] - acst[ci:ci + 1, :]
                lm = jnp.exp(jnp.where(keep, seg, NEG_BIG))
                ms.append((cb * lm).astype(BF16))
            lhs = jnp.concatenate(ms, axis=1)
            x2 = xcb[:, j * LANES:(j + 1) * LANES]
            zero = jnp.zeros_like(x2)
            rhs = jnp.concatenate([jnp.where(first_half, x2, zero),
                                   jnp.where(first_half, zero, x2)], axis=0)
            y = _dot(lhs, rhs) + y_off[:, jj * LANES:(jj + 1) * LANES]
            if direction == 0:
                cs = slice(j * LANES, (j + 1) * LANES)
                y = y + xs[:, cs] * dskip_ref[:, cs]
            y_ref[:, j * LANES:(j + 1) * LANES] = y


def _ssd_kernel(xs_f, b_f, bt_f, c_f, dt_f, dtt_f, xs_b, b_b, bt_b, c_b, dt_b, dtt_b,
                bias_ref, biast_ref, alog_ref, alogt_ref, dskip_ref, rep_ref,
                yf_ref, yb_ref, sf_ref, sb_ref):
    @pl.when(pl.program_id(1) == 0)
    def _():
        sf_ref[...] = jnp.zeros_like(sf_ref)
        sb_ref[...] = jnp.zeros_like(sb_ref)

    _ssd_direction(0, xs_f, b_f, bt_f, c_f, dt_f, dtt_f, bias_ref, biast_ref, alog_ref, alogt_ref,
                   dskip_ref, rep_ref.at[0], yf_ref, sf_ref)
    _ssd_direction(1, xs_b, b_b, bt_b, c_b, dt_b, dtt_b, bias_ref, biast_ref, alog_ref, alogt_ref,
                   dskip_ref, rep_ref.at[1], yb_ref, sb_ref)


def _ssd_scan(xs, bm, bmt, cm, dt, dtt, dt_bias, a_log, d_skip):
    b, s, _ = xs.shape
    q = SSD_CHUNK
    nc = s // q
    nh2 = 2 * SSD_HEADS

    def fwd(bi, c):
        return c

    def bwd(bi, c):
        return nc - 1 - c

    def chunk_specs(cidx):
        return [pl.BlockSpec((None, q, D_INNER), lambda bi, c: (bi, cidx(bi, c), 0)),
                pl.BlockSpec((None, q, GN), lambda bi, c: (bi, cidx(bi, c), 0)),
                pl.BlockSpec((None, GN, q), lambda bi, c: (bi, 0, cidx(bi, c))),
                pl.BlockSpec((None, q, GN), lambda bi, c: (bi, cidx(bi, c), 0)),
                pl.BlockSpec((None, q, nh2), lambda bi, c: (bi, cidx(bi, c), 0)),
                pl.BlockSpec((nh2, q), lambda bi, c: (0, bi * nc + cidx(bi, c)))]

    def const(shape):
        return pl.BlockSpec(shape, lambda bi, c: (0,) * len(shape))

    rep = np.zeros((2, 4 * nh2, D_INNER), np.float32)
    for d in range(2):
        for h in range(SSD_HEADS):
            for piece in range(3):
                rep[d, piece * nh2 + d * SSD_HEADS + h, h * SSD_HEADDIM:(h + 1) * SSD_HEADDIM] = 1.0
    rep = jnp.asarray(rep, BF16)
    dskip_e = jnp.repeat(d_skip.astype(F32), SSD_HEADDIM).reshape(1, D_INNER)
    bias = dt_bias.reshape(1, nh2).astype(F32)
    alog = a_log.reshape(1, nh2).astype(F32)

    ins = (xs, bm, bmt, cm, dt.reshape(b, s, nh2), dtt)
    return pl.pallas_call(
        _ssd_kernel,
        grid=(b, nc),
        in_specs=chunk_specs(fwd) + chunk_specs(bwd) + [
            const((1, nh2)), const((nh2, 1)), const((1, nh2)), const((nh2, 1)),
            const((1, D_INNER)), const((2, 4 * nh2, D_INNER))],
        out_specs=[pl.BlockSpec((None, q, D_INNER), lambda bi, c: (bi, c, 0)),
                   pl.BlockSpec((None, q, D_INNER), lambda bi, c: (bi, nc - 1 - c, 0))],
        out_shape=[jax.ShapeDtypeStruct((b, s, D_INNER), F32),
                   jax.ShapeDtypeStruct((b, s, D_INNER), F32)],
        scratch_shapes=[pltpu.VMEM((SSD_STATE, D_INNER), F32),
                        pltpu.VMEM((SSD_STATE, D_INNER), F32)],
        compiler_params=_cparams("parallel", "arbitrary"),
        name="ssd_scan",
    )(*ins, *ins, bias, bias.reshape(nh2, 1), alog, alog.reshape(nh2, 1), dskip_e, rep)


def _flash_kernel(q_ref, k_ref, v_ref, o_ref, qa_ref, ka_ref, vat_ref, kmax_ref, acc_ref, m_ref, l_ref,
                  *, bq, bk, bk_slow):
    seq = k_ref.shape[0]
    d = HEAD_DIM
    rows = GQA_REP * bq
    ones_sq = jnp.ones((d, d), BF16)
    first_lane = lax.broadcasted_iota(jnp.int32, (1, d), 1) == 0

    @pl.when(pl.program_id(2) == 0)
    def _():
        kmax_ref[...] = jnp.zeros_like(kmax_ref)
        one_col = jnp.where(first_lane, 1.0, 0.0).astype(BF16)
        first_row = lax.broadcasted_iota(jnp.int32, (VT_ROWS - d, 1), 0) == 0
        one_row = jnp.where(first_row, 1.0, 0.0).astype(BF16)

        def fill(c, carry):
            off = pl.multiple_of(c * bk_slow, bk_slow)
            kb = k_ref[pl.ds(off, bk_slow), :]
            ka_ref[pl.ds(off, bk_slow), 0:d] = kb
            ka_ref[pl.ds(off, bk_slow), d:2 * d] = jnp.broadcast_to(one_col, (bk_slow, d))
            vat_ref[0:d, pl.ds(off, bk_slow)] = v_ref[pl.ds(off, bk_slow), :].astype(F32).T.astype(BF16)
            vat_ref[d:VT_ROWS, pl.ds(off, bk_slow)] = jnp.broadcast_to(one_row, (VT_ROWS - d, bk_slow))
            kf = kb.astype(F32)
            kn2 = _dot((kf * kf).astype(BF16), ones_sq)
            kmax_ref[...] = jnp.maximum(kmax_ref[...], jnp.max(kn2, axis=0, keepdims=True))
            return carry

        lax.fori_loop(0, seq // bk_slow, fill, 0)

    for r in range(GQA_REP):
        qa_ref[r * bq:(r + 1) * bq, 0:d] = q_ref[:, r * d:(r + 1) * d]
    qf = qa_ref[:, 0:d].astype(F32)
    qn2 = _dot((qf * qf).astype(BF16), ones_sq)
    shift = jnp.sqrt(qn2 * kmax_ref[...]) * NORM_SLACK
    qa_ref[:, d:2 * d] = jnp.where(first_lane, -shift, 0.0).astype(BF16)
    use_fast = jnp.max(shift) * 2.0 <= FAST_SOFTMAX_LOG2_RANGE

    @pl.when(use_fast)
    def _():
        qa = qa_ref[...]
        acc_t = jnp.zeros((VT_ROWS, rows), F32)
        for u in range(seq // bk):
            ks = slice(u * bk, (u + 1) * bk)
            s_t = _dot_nt(ka_ref[ks, :], qa)
            acc_t = acc_t + _dot(vat_ref[:, ks], jnp.exp2(s_t).astype(BF16))
        out = (acc_t[0:d, :] * (1.0 / acc_t[d:d + 1, :])).T
        for r in range(GQA_REP):
            o_ref[:, r * d:(r + 1) * d] = out[r * bq:(r + 1) * bq, :].astype(o_ref.dtype)

    @pl.when(jnp.logical_not(use_fast))
    def _():
        m_ref[...] = jnp.full_like(m_ref, NEG_BIG)
        l_ref[...] = jnp.zeros_like(l_ref)
        acc_ref[...] = jnp.zeros_like(acc_ref)

        def body(j, carry):
            off = pl.multiple_of(j * bk_slow, bk_slow)
            s = _dot_nt(qa_ref[:, 0:d], k_ref[pl.ds(off, bk_slow), :])
            m_prev = m_ref[...]
            m_new = jnp.maximum(m_prev, jnp.max(s, axis=-1, keepdims=True))
            alpha = jnp.exp2(m_prev - m_new)
            p = jnp.exp2(s - m_new)
            l_ref[...] = alpha * l_ref[...] + jnp.sum(p, axis=-1, keepdims=True)
            acc_ref[...] = alpha * acc_ref[...] + _dot(p.astype(BF16), v_ref[pl.ds(off, bk_slow), :])
            m_ref[...] = m_new
            return carry

        lax.fori_loop(0, seq // bk_slow, body, 0)
        inv = 1.0 / l_ref[...]
        for r in range(GQA_REP):
            rs = slice(r * bq, (r + 1) * bq)
            o_ref[:, r * d:(r + 1) * d] = (acc_ref[rs, :] * inv[rs, :]).astype(o_ref.dtype)


def _flash_gqa(q, k, v, bq=256, bk=512, bk_slow=1024):
    b, s, _ = q.shape
    bk = min(bk, s)
    bk_slow = min(bk_slow, s)
    rows = GQA_REP * bq
    gw = GQA_REP * HEAD_DIM
    return pl.pallas_call(
        functools.partial(_flash_kernel, bq=bq, bk=bk, bk_slow=bk_slow),
        grid=(b, GQA_KV_HEADS, s // bq),
        in_specs=[pl.BlockSpec((None, bq, gw), lambda bi, kh, i: (bi, i, kh)),
                  pl.BlockSpec((None, s, HEAD_DIM), lambda bi, kh, i: (bi, 0, kh)),
                  pl.BlockSpec((None, s, HEAD_DIM), lambda bi, kh, i: (bi, 0, kh))],
        out_specs=pl.BlockSpec((None, bq, gw), lambda bi, kh, i: (bi, i, kh)),
        out_shape=jax.ShapeDtypeStruct((b, s, GQA_WIDTH), BF16),
        scratch_shapes=[pltpu.VMEM((rows, 2 * HEAD_DIM), BF16),
                        pltpu.VMEM((s, 2 * HEAD_DIM), BF16),
                        pltpu.VMEM((VT_ROWS, s), BF16),
                        pltpu.VMEM((1, HEAD_DIM), F32),
                        pltpu.VMEM((rows, HEAD_DIM), F32),
                        pltpu.VMEM((rows, 1), F32),
                        pltpu.VMEM((rows, 1), F32)],
        compiler_params=_cparams("parallel", "parallel", "arbitrary"),
        name="flash_gqa",
    )(q, k, v)


def _band_kernel(q_ref, kp_ref, kc_ref, kn_ref, vp_ref, vc_ref, vn_ref, bias_ref, o_ref, lse_ref,
                 *, dil, nsub, length):
    i = pl.program_id(1)
    tq = BAND_TILE
    col = lax.broadcasted_iota(jnp.int32, (tq, 2 * tq), 1)
    for u in range(nsub):
        kpos = (i * nsub + u) * tq - DIL_HALF + col
        inside = (kpos >= 0) & (kpos < length)
        rows = slice(u * tq, (u + 1) * tq)
        head = slice(u * tq - DIL_HALF, u * tq)
        tail = slice((u + 1) * tq, (u + 1) * tq + DIL_HALF)
        for r in range(dil):
            for hd in range(DIL_HEADS_PER_GROUP):
                c0 = (r * DIL_HEADS_PER_GROUP + hd) * HEAD_DIM
                sl = slice(c0, c0 + HEAD_DIM)

                def window(prev_ref, cur_ref, next_ref):
                    first = prev_ref[:, sl] if u == 0 else cur_ref[head, sl]
                    last = next_ref[:, sl] if u == nsub - 1 else cur_ref[tail, sl]
                    return jnp.concatenate([first, cur_ref[rows, sl], last], axis=0)

                s = _dot_nt(q_ref[rows, sl], window(kp_ref, kc_ref, kn_ref)) + bias_ref[hd]
                s = jnp.where(inside, s, NEG_BIG)
                m = jnp.max(s, axis=-1, keepdims=True)
                p = jnp.exp(s - m)
                l = jnp.sum(p, axis=-1, keepdims=True)
                o_ref[rows, sl] = _dot(p.astype(BF16), window(vp_ref, vc_ref, vn_ref)) / l
                lse_ref[rows, sl] = jnp.broadcast_to(m + jnp.log(l), (tq, HEAD_DIM))


def _band_attention(qr, kr, vr, bias_t, group, dil, batch):
    rows_total, width = qr.shape
    length = rows_total // batch
    nsub = max(1, BAND_TILES_PER_STEP // dil)
    bqr = nsub * BAND_TILE
    nb = length // bqr
    hb = bqr // DIL_HALF
    last_hb = length // DIL_HALF - 1
    view = lambda t: t.reshape(batch, length, width)

    cur = pl.BlockSpec((None, bqr, width), lambda bi, i: (bi, i, 0))
    prev = pl.BlockSpec((None, DIL_HALF, width), lambda bi, i: (bi, jnp.maximum(i * hb - 1, 0), 0))
    nxt = pl.BlockSpec((None, DIL_HALF, width), lambda bi, i: (bi, jnp.minimum((i + 1) * hb, last_hb), 0))
    o, lse = pl.pallas_call(
        functools.partial(_band_kernel, dil=dil, nsub=nsub, length=length),
        grid=(batch, nb),
        in_specs=[cur, prev, cur, nxt, prev, cur, nxt,
                  pl.BlockSpec((DIL_HEADS_PER_GROUP, BAND_TILE, 2 * BAND_TILE), lambda bi, i: (0, 0, 0))],
        out_specs=[cur, cur],
        out_shape=[jax.ShapeDtypeStruct((batch, length, width), F32),
                   jax.ShapeDtypeStruct((batch, length, width), F32)],
        compiler_params=_cparams("parallel", "parallel"),
        name=f"band_attn_g{group}",
    )(view(qr), view(kr), view(kr), view(kr), view(vr), view(vr), view(vr), bias_t)
    return o.reshape(rows_total, width), lse.reshape(rows_total, width)


def _t5_bucket(rel):
    nb = N_BUCKETS // 2
    max_exact = nb // 2
    ret = jnp.where(rel > 0, nb, 0)
    n = jnp.abs(rel)
    nf = jnp.maximum(n, 1).astype(F32)
    large = max_exact + (jnp.log(nf / max_exact) / math.log(REL_MAX_DIST / max_exact)
                         * (nb - max_exact)).astype(jnp.int32)
    large = jnp.minimum(large, nb - 1)
    return ret + jnp.where(n < max_exact, n, large)


def _band_bias_tiles(rel_bias):
    tq = BAND_TILE
    period = 3 * tq
    tiles = []
    for g, (_, dil) in enumerate(DIL_PATTERNS):
        dist = jnp.arange(-DIL_HALF, DIL_HALF + 1, dtype=jnp.int32) * dil
        tbl = rel_bias.astype(F32)[_t5_bucket(dist)]
        tbl = tbl[:, g * DIL_HEADS_PER_GROUP:(g + 1) * DIL_HEADS_PER_GROUP].T
        nh = tbl.shape[0]
        u = jnp.concatenate([jnp.full((nh, tq - 1), NEG_BIG, F32), tbl,
                             jnp.full((nh, period - tq - 2 * DIL_HALF), NEG_BIG, F32)], axis=1)
        rows = jnp.tile(u, (1, tq))[:, :tq * (period - 1)].reshape(nh, tq, period - 1)
        tiles.append(rows[:, :, tq - 1:3 * tq - 1])
    return tiles


def _to_token_order(src_ref, scr_ref, dil):
    if dil == 1:
        return src_ref[...]
    n = src_ref.shape[0]
    nlt = DIL_OUT // LANES
    for r in range(dil):
        for c in range(nlt):
            c0 = r * DIL_OUT + c * LANES
            scr_ref[c, pl.ds(r, n, stride=dil), :] = src_ref[:, c0:c0 + LANES]
    return jnp.concatenate([scr_ref[c] for c in range(nlt)], axis=1)


def _merge_kernel(x_ref, yf_ref, yb_ref, zs_ref, gssd_ref, ygqa_ref,
                  o0_ref, l0_ref, o1_ref, l1_ref, o2_ref, l2_ref, gates_ref,
                  wssd_ref, wgqa_ref, wdil_ref, wout_ref, gpost_ref, o_ref, *scr):
    y = (yf_ref[...] + yb_ref[...]) * zs_ref[...]
    y_ssd = _rms(y, gssd_ref[...]).astype(BF16)
    dils = [dil for _, dil in DIL_PATTERNS]
    o0, o1, o2 = (_to_token_order(r, s, dl) for r, s, dl in zip((o0_ref, o1_ref, o2_ref), scr[0:3], dils))
    l0, l1, l2 = (_to_token_order(r, s, dl) for r, s, dl in zip((l0_ref, l1_ref, l2_ref), scr[3:6], dils))
    mx = jnp.maximum(jnp.maximum(l0, l1), l2)
    e0, e1, e2 = jnp.exp(l0 - mx), jnp.exp(l1 - mx), jnp.exp(l2 - mx)
    y_dil = ((e0 * o0 + e1 * o1 + e2 * o2) / (e0 + e1 + e2)).astype(BF16)
    d = D_MODEL
    mix = gates_ref[:, 0:d] * _dot(y_ssd, wssd_ref[...])
    mix = mix + gates_ref[:, d:2 * d] * _dot(ygqa_ref[...], wgqa_ref[...])
    mix = mix + gates_ref[:, 2 * d:3 * d] * _dot(y_dil, wdil_ref[...])
    out = _dot(mix.astype(BF16), wout_ref[...])
    o_ref[...] = x_ref[...] + _rms(out, gpost_ref[...])


def _merge(x2d, yf, yb, zs, g_ssd, y_gqa, dil_parts, gates, w_ssd, w_gqa, w_dil, w_out, g_post, bm=256):
    t, d = x2d.shape
    row = lambda n: pl.BlockSpec((bm, n), lambda i: (i, 0))
    const = lambda a: pl.BlockSpec(a.shape, lambda i: (0, 0))
    (o0, l0), (o1, l1), (o2, l2) = dil_parts
    g_ssd = g_ssd.reshape(1, -1)
    g_post = g_post.reshape(1, -1)
    args = (x2d, yf, yb, zs, g_ssd, y_gqa, o0, l0, o1, l1, o2, l2, gates, w_ssd, w_gqa, w_dil, w_out, g_post)
    specs = [row(d), row(D_INNER), row(D_INNER), row(D_INNER), const(g_ssd), row(GQA_WIDTH)]
    for _, dil in DIL_PATTERNS:
        specs += [pl.BlockSpec((bm // dil, dil * DIL_OUT), lambda i: (i, 0))] * 2
    specs += [row(N_BRANCHES * d), const(w_ssd), const(w_gqa), const(w_dil), const(w_out), const(g_post)]
    return pl.pallas_call(
        _merge_kernel,
        grid=(t // bm,),
        in_specs=specs,
        out_specs=row(d),
        out_shape=jax.ShapeDtypeStruct((t, d), F32),
        scratch_shapes=[pltpu.VMEM((DIL_OUT // LANES, bm, LANES), F32)] * 6,
        compiler_params=_cparams("parallel"),
        name="merge_out",
    )(*args)


def _ffn_kernel(x_ref, pa_ref, pb_ref, gpre_ref, wg_ref, wu_ref, wd_ref, gpost_ref, gple_ref, wpg_ref, wp_ref,
                gnext_ref, *o_refs, nc, na, split_output):
    x = x_ref[...]
    h = _rms(x, gpre_ref[...]).astype(BF16)
    ff = jnp.zeros(x.shape, F32)
    for c in range(0, wg_ref.shape[1], nc):
        a = _silu(_dot(h, wg_ref[:, c:c + nc])) * _dot(h, wu_ref[:, c:c + nc])
        ff = ff + _dot(a.astype(BF16), wd_ref[c:c + nc, :])
    x = x + _rms(ff, gpost_ref[...])
    hn = _rms(x, gple_ref[...]).astype(BF16)
    gate = _sigmoid(_dot(hn, wpg_ref[...]))

    def emit(p_ref, o_ref):
        out = x + _dot(p_ref[...].astype(BF16), wp_ref[...]) * gate
        o_ref[...] = out
        if not split_output:
            o_refs[1][...] = _rms(out, gnext_ref[...]).astype(o_refs[1].dtype)

    pl.when(pl.program_id(0) < na)(lambda: emit(pa_ref, o_refs[0]))
    pl.when(pl.program_id(0) >= na)(lambda: emit(pb_ref, o_refs[1] if split_output else o_refs[0]))


def _ffn_ple(x2d, pa, pb, layer, g_pre, w_gate, w_up, w_down, g_post, g_ple, w_ple_gate, w_ple, g_next,
             split_output, bm=256):
    t, d = x2d.shape
    ta, tb = pa.shape[1], pb.shape[1]
    na = ta // bm
    row = lambda n: pl.BlockSpec((bm, n), lambda i: (i, 0))
    first = lambda i: jnp.minimum(i, na - 1)
    second = lambda i: jnp.maximum(i - na, 0)
    const = lambda a: pl.BlockSpec(a.shape, lambda i: (0, 0))
    g_pre, g_post, g_ple, g_next = (g.reshape(1, d) for g in (g_pre, g_post, g_ple, g_next))
    consts = (g_pre, w_gate, w_up, w_down, g_post, g_ple, w_ple_gate, w_ple, g_next)
    specs = [row(d),
             pl.BlockSpec((None, bm, PLE_DIM), lambda i: (layer, first(i), 0)),
             pl.BlockSpec((None, bm, PLE_DIM), lambda i: (layer, second(i), 0))]
    specs += [const(a) for a in consts]
    if split_output:
        out_specs = [pl.BlockSpec((bm, d), lambda i: (first(i), 0)),
                     pl.BlockSpec((bm, d), lambda i: (second(i), 0))]
        out_shape = [jax.ShapeDtypeStruct((ta, d), F32), jax.ShapeDtypeStruct((tb, d), F32)]
    else:
        out_specs = [row(d), row(d)]
        out_shape = [jax.ShapeDtypeStruct((t, d), F32), jax.ShapeDtypeStruct((t, d), BF16)]
    nc = 256 if FFN_DIM % 256 == 0 else FFN_DIM
    return pl.pallas_call(
        functools.partial(_ffn_kernel, nc=nc, na=na, split_output=split_output),
        grid=(t // bm,),
        in_specs=specs,
        out_specs=out_specs,
        out_shape=out_shape,
        compiler_params=_cparams("arbitrary"),
        name="ffn_ple",
    )(x2d, pa, pb, *consts)


def _rope_tables(seq):
    pos = jnp.arange(seq)
    row = (pos // GRID_W).astype(F32)
    colp = (pos % GRID_W).astype(F32)
    n_pairs = HEAD_DIM // 4
    inv = ROPE_THETA ** (-jnp.arange(n_pairs, dtype=F32) / n_pairs)
    ang = jnp.concatenate([row[:, None] * inv, colp[:, None] * inv], axis=-1)
    c, s = jnp.cos(ang), jnp.sin(ang)
    return jnp.concatenate([c, c], axis=-1), jnp.concatenate([-s, s], axis=-1)


def _layer(h, x2d, b, s, pa, pb, i, P, cosf, sinf, bias_tiles, last):
    t = b * s
    offs = np.concatenate([[0], np.cumsum(IN_WIDTHS)])
    w_in = P["w_in"][i]
    wz, wxbc, wdt, wq, wk, wv, wdq, wdk, wdv, wgates = (
        w_in[:, offs[j]:offs[j + 1]] for j in range(len(IN_WIDTHS)))
    def deinterleave(w):
        lead = w.shape[:-1]
        return w.reshape(*lead, -1, HEAD_DIM // 2, 2).swapaxes(-1, -2).reshape(*lead, -1)

    wq, wk = deinterleave(wq), deinterleave(wk)
    bf = lambda w: w.astype(BF16)

    zs = _proj(h, bf(wz), BF16, act="silu", name="proj_z")
    xs, bm, bmt, cm = _proj_conv(h, bf(wxbc), P["conv_w"][i], P["conv_b"][i], b)
    dt, dtt = _proj_dt(h, bf(wdt))
    q = _proj_qk(h, bf(wq), deinterleave(P["g_q"][i]), cosf, sinf, ATTN_SCALE * LOG2E, s, name="proj_q")
    k = _proj_qk(h, bf(wk), deinterleave(P["g_k"][i]), cosf, sinf, 1.0, s, name="proj_k")
    v = _proj(h, bf(wv), BF16, name="proj_v")
    dil_qkv = _proj_dil(h, bf(jnp.concatenate([wdq, wdk, wdv], axis=1)))
    gates = _proj(h, bf(wgates), BF16, act="sigmoid", name="proj_gates")

    yf, yb = _ssd_scan(xs.reshape(b, s, D_INNER), bm.reshape(b, s, GN), bmt, cm.reshape(b, s, GN), dt, dtt,
                       P["dt_bias"][i], P["a_log"][i], P["d_skip"][i])

    y_gqa = _flash_gqa(q.reshape(b, s, GQA_WIDTH), k.reshape(b, s, GQA_KV_WIDTH),
                       v.reshape(b, s, GQA_KV_WIDTH))

    dil_parts = []
    ng = len(DIL_PATTERNS)
    for g, (_, dil) in enumerate(DIL_PATTERNS):
        dil_parts.append(_band_attention(dil_qkv[g], dil_qkv[ng + g], dil_qkv[2 * ng + g],
                                         bias_tiles[g], g, dil, b))

    x2d = _merge(x2d, yf.reshape(t, D_INNER), yb.reshape(t, D_INNER), zs, P["g_ssd"][i],
                 y_gqa.reshape(t, GQA_WIDTH), dil_parts, gates,
                 bf(P["w_br_ssd"][i]), bf(P["w_br_gqa"][i]), bf(P["w_br_dil"][i]), bf(P["w_out"][i]),
                 P["g_post_mix"][i])
    g_next = P["g_pre_mix"][i if last else i + 1]
    return _ffn_ple(x2d, pa, pb, i, P["g_pre_ffn"][i], bf(P["w_gate"][i]), bf(P["w_up"][i]),
                    bf(P["w_down"][i]), P["g_post_ffn"][i], P["g_ple"][i], bf(P["w_ple_gate"][i]),
                    bf(P["w_ple"][i]), g_next, split_output=last)


def kernel(x_prompt, x_sample, p_prompt, p_sample, w_in, conv_w, conv_b, dt_bias, a_log, d_skip, g_ssd, g_q, g_k, w_br_ssd, w_br_gqa, w_br_dil, w_out, g_pre_mix, g_post_mix, g_pre_ffn, g_post_ffn, w_gate, w_up, w_down, w_ple, g_ple, w_ple_gate, rel_bias):
    P = dict(w_in=w_in, conv_w=conv_w, conv_b=conv_b, dt_bias=dt_bias, a_log=a_log, d_skip=d_skip,
             g_ssd=g_ssd, g_q=g_q, g_k=g_k, w_br_ssd=w_br_ssd, w_br_gqa=w_br_gqa, w_br_dil=w_br_dil,
             w_out=w_out, g_pre_mix=g_pre_mix, g_post_mix=g_post_mix, g_pre_ffn=g_pre_ffn,
             g_post_ffn=g_post_ffn, w_gate=w_gate, w_up=w_up, w_down=w_down, w_ple=w_ple,
             g_ple=g_ple, w_ple_gate=w_ple_gate)
    ba, s, d = x_prompt.shape
    bb = x_sample.shape[0]
    assert x_sample.shape[1:] == (s, d), "both request groups must have the same sequence length"
    depth = p_prompt.shape[0]
    pa = p_prompt.reshape(depth, ba * s, PLE_DIM)
    pb = p_sample.reshape(depth, bb * s, PLE_DIM)
    cosf, sinf = _rope_tables(s)
    bias_tiles = _band_bias_tiles(rel_bias)
    h, x2d = _prenorm_cat(x_prompt.reshape(ba * s, d), x_sample.reshape(bb * s, d), g_pre_mix[0])
    for i in range(depth - 1):
        x2d, h = _layer(h, x2d, ba + bb, s, pa, pb, i, P, cosf, sinf, bias_tiles, last=False)
    ya, yb = _layer(h, x2d, ba + bb, s, pa, pb, depth - 1, P, cosf, sinf, bias_tiles, last=True)
    return (ya.reshape(ba, s, d), yb.reshape(bb, s, d))
```

```python
import functools
import math

import jax
import jax.numpy as jnp
import numpy as np
from jax import lax
from jax.experimental import pallas as pl
from jax.experimental.pallas import tpu as pltpu

F32 = jnp.float32
BF16 = jnp.bfloat16

D_MODEL = 1024
GRID_W = 64
PLE_DIM = 256
EPS = 1e-6
SSD_HEADS = 32
SSD_HEADDIM = 64
D_INNER = SSD_HEADS * SSD_HEADDIM
SSD_GROUPS = 4
SSD_STATE = 128
SSD_CHUNK = 128
CONV_W = 5
GN = SSD_GROUPS * SSD_STATE
XBC_DIM = D_INNER + 2 * GN
HEAD_DIM = 128
GQA_Q_HEADS = 16
GQA_KV_HEADS = 4
GQA_REP = GQA_Q_HEADS // GQA_KV_HEADS
GQA_WIDTH = GQA_Q_HEADS * HEAD_DIM
GQA_KV_WIDTH = GQA_KV_HEADS * HEAD_DIM
ROPE_THETA = 10000.0
ATTN_SCALE = HEAD_DIM ** -0.5
DIL_PATTERNS = ((128, 1), (512, 4), (2048, 16))
DIL_HEADS_PER_GROUP = 4
DIL_HEADS = len(DIL_PATTERNS) * DIL_HEADS_PER_GROUP
DIL_WIDTH = DIL_HEADS * HEAD_DIM
DIL_OUT = DIL_HEADS_PER_GROUP * HEAD_DIM
DIL_HALF = 64
BAND_TILE = 2 * DIL_HALF
BAND_TILES_PER_STEP = 16
N_BUCKETS = 32
REL_MAX_DIST = 2048
FFN_DIM = ((8 * D_MODEL // 3 + 255) // 256) * 256
N_BRANCHES = 3
IN_WIDTHS = (D_INNER, XBC_DIM, 2 * SSD_HEADS, GQA_WIDTH, GQA_KV_WIDTH, GQA_KV_WIDTH,
             DIL_WIDTH, DIL_WIDTH, DIL_WIDTH, N_BRANCHES * D_MODEL)

LANES = 128
SUBLANES = 8
BF16_ROWS = 16
MXU_COLS = 256
VMEM_LIMIT = 56 * 1024 * 1024
NEG_BIG = -1e30
LOG2E = math.log2(math.e)
VT_ROWS = HEAD_DIM + BF16_ROWS
NORM_SLACK = 1.02
FAST_SOFTMAX_LOG2_RANGE = 100.0


def _cparams(*sem):
    return pltpu.CompilerParams(dimension_semantics=sem, vmem_limit_bytes=VMEM_LIMIT)


def _rms(x, g):
    ms = jnp.mean(x * x, axis=-1, keepdims=True)
    return x * lax.rsqrt(ms + EPS) * g


def _sigmoid(x):
    return 1.0 / (1.0 + jnp.exp(-x))


def _silu(x):
    return x * _sigmoid(x)


def _softplus(x):
    return jnp.maximum(x, 0.0) + jnp.log1p(jnp.exp(-jnp.abs(x)))


def _dot(a, b):
    return jnp.dot(a, b, preferred_element_type=F32)


def _dot_nt(a, b):
    return lax.dot_general(a, b, (((1,), (1,)), ((), ())), preferred_element_type=F32)


def _split3(x):
    hi = x.astype(BF16)
    r1 = x - hi.astype(F32)
    mid = r1.astype(BF16)
    lo = (r1 - mid.astype(F32)).astype(BF16)
    return hi, mid, lo


def _dot_exact_lhs(x, m_bf16, pieces=3):
    parts = _split3(x)[:pieces]
    acc = _dot(parts[0], m_bf16)
    for p in parts[1:]:
        acc = acc + _dot(p, m_bf16)
    return acc


def _dot_exact_rhs(m_bf16, x, pieces=3):
    parts = _split3(x)[:pieces]
    acc = _dot(m_bf16, parts[0])
    for p in parts[1:]:
        acc = acc + _dot(m_bf16, p)
    return acc


def _prenorm_cat_kernel(xa_ref, xb_ref, g_ref, h_ref, x_ref, *, na):
    def emit(src_ref):
        x = src_ref[...]
        h_ref[...] = _rms(x, g_ref[...]).astype(h_ref.dtype)
        x_ref[...] = x

    pl.when(pl.program_id(0) < na)(lambda: emit(xa_ref))
    pl.when(pl.program_id(0) >= na)(lambda: emit(xb_ref))


def _prenorm_cat(xa, xb, g, bm=1024):
    (ta, d), tb = xa.shape, xb.shape[0]
    na = ta // bm
    t = ta + tb
    return pl.pallas_call(
        functools.partial(_prenorm_cat_kernel, na=na),
        grid=(t // bm,),
        in_specs=[pl.BlockSpec((bm, d), lambda i: (jnp.minimum(i, na - 1), 0)),
                  pl.BlockSpec((bm, d), lambda i: (jnp.maximum(i - na, 0), 0)),
                  pl.BlockSpec((1, d), lambda i: (0, 0))],
        out_specs=[pl.BlockSpec((bm, d), lambda i: (i, 0)),
                   pl.BlockSpec((bm, d), lambda i: (i, 0))],
        out_shape=[jax.ShapeDtypeStruct((t, d), BF16),
                   jax.ShapeDtypeStruct((t, d), F32)],
        compiler_params=_cparams("arbitrary"),
        name="prenorm_cat",
    )(xa, xb, g.reshape(1, d))


def _proj_kernel(h_ref, w_ref, o_ref, *, act, nc):
    h = h_ref[...]
    n = o_ref.shape[-1]
    for c in range(0, n, nc):
        y = _dot(h, w_ref[:, c:c + nc])
        if act == "silu":
            y = _silu(y)
        elif act == "sigmoid":
            y = _sigmoid(y)
        o_ref[:, c:c + nc] = y.astype(o_ref.dtype)


def _proj(h, w, out_dtype, act=None, bm=1024, name="proj"):
    t, d = h.shape
    n = w.shape[1]
    nc = 512 if n % 512 == 0 else n
    return pl.pallas_call(
        functools.partial(_proj_kernel, act=act, nc=nc),
        grid=(t // bm,),
        in_specs=[pl.BlockSpec((bm, d), lambda i: (i, 0)),
                  pl.BlockSpec((d, n), lambda i: (0, 0))],
        out_specs=pl.BlockSpec((bm, n), lambda i: (i, 0)),
        out_shape=jax.ShapeDtypeStruct((t, n), out_dtype),
        compiler_params=_cparams("parallel"),
        name=name,
    )(h, w)


def _proj_qk_kernel(h_ref, w_ref, g_ref, cos_ref, sin_ref, o_ref, *, scale):
    h = h_ref[...]
    g = g_ref[...]
    cosf = cos_ref[...]
    sinf = sin_ref[...]
    for hd in range(o_ref.shape[-1] // HEAD_DIM):
        sl = slice(hd * HEAD_DIM, (hd + 1) * HEAD_DIM)
        y = _rms(_dot(h, w_ref[:, sl]), g)
        y = y * cosf + pltpu.roll(y, HEAD_DIM // 2, 1) * sinf
        if scale != 1.0:
            y = y * scale
        o_ref[:, sl] = y.astype(o_ref.dtype)


def _proj_qk(h, w, g, cosf, sinf, scale, seq, bm=512, name="proj_qk"):
    t, d = h.shape
    n = w.shape[1]
    nsb = seq // bm
    return pl.pallas_call(
        functools.partial(_proj_qk_kernel, scale=scale),
        grid=(t // bm,),
        in_specs=[pl.BlockSpec((bm, d), lambda i: (i, 0)),
                  pl.BlockSpec((d, n), lambda i: (0, 0)),
                  pl.BlockSpec((1, HEAD_DIM), lambda i: (0, 0)),
                  pl.BlockSpec((bm, HEAD_DIM), lambda i: (i % nsb, 0)),
                  pl.BlockSpec((bm, HEAD_DIM), lambda i: (i % nsb, 0))],
        out_specs=pl.BlockSpec((bm, n), lambda i: (i, 0)),
        out_shape=jax.ShapeDtypeStruct((t, n), BF16),
        compiler_params=_cparams("parallel"),
        name=name,
    )(h, w, g.reshape(1, HEAD_DIM), cosf, sinf)


def _proj_dil_kernel(h_ref, w_ref, *refs):
    outs, scr = refs[:-1], refs[-1]
    h = h_ref[...]
    bm = h.shape[0]
    nlt = DIL_OUT // LANES
    for t in range(3):
        for g, (_, dil) in enumerate(DIL_PATTERNS):
            o_ref = outs[t * len(DIL_PATTERNS) + g]
            c0 = (t * len(DIL_PATTERNS) + g) * DIL_OUT
            y = _dot(h, w_ref[:, c0:c0 + DIL_OUT])
            if t == 0:
                y = y * ATTN_SCALE
            if dil == 1:
                o_ref[...] = y.astype(o_ref.dtype)
                continue
            for c in range(nlt):
                scr[c] = y[:, c * LANES:(c + 1) * LANES]
            for r in range(dil):
                for c in range(nlt):
                    o0 = r * DIL_OUT + c * LANES
                    o_ref[:, o0:o0 + LANES] = scr[c, pl.ds(r, bm // dil, stride=dil), :].astype(o_ref.dtype)


def _proj_dil(h, w, bm=512):
    t, d = h.shape
    n = w.shape[1]
    out_specs, out_shapes = [], []
    for _ in range(3):
        for _, dil in DIL_PATTERNS:
            out_specs.append(pl.BlockSpec((bm // dil, dil * DIL_OUT), lambda i: (i, 0)))
            out_shapes.append(jax.ShapeDtypeStruct((t // dil, dil * DIL_OUT), BF16))
    return pl.pallas_call(
        _proj_dil_kernel,
        grid=(t // bm,),
        in_specs=[pl.BlockSpec((bm, d), lambda i: (i, 0)),
                  pl.BlockSpec((d, n), lambda i: (0, 0))],
        out_specs=out_specs,
        out_shape=out_shapes,
        scratch_shapes=[pltpu.VMEM((DIL_OUT // LANES, bm, LANES), F32)],
        compiler_params=_cparams("parallel"),
        name="proj_dil",
    )(h, w)


def _proj_dt_kernel(h_ref, w_ref, wt_ref, o_ref, ot_ref):
    h = h_ref[...]
    o_ref[...] = _dot(h, w_ref[...])
    ot_ref[...] = _dot_nt(wt_ref[...], h)


def _proj_dt(h, w, bm=512):
    t, d = h.shape
    n = w.shape[1]
    return pl.pallas_call(
        _proj_dt_kernel,
        grid=(t // bm,),
        in_specs=[pl.BlockSpec((bm, d), lambda i: (i, 0)),
                  pl.BlockSpec((d, n), lambda i: (0, 0)),
                  pl.BlockSpec((n, d), lambda i: (0, 0))],
        out_specs=[pl.BlockSpec((bm, n), lambda i: (i, 0)),
                   pl.BlockSpec((n, bm), lambda i: (0, i))],
        out_shape=[jax.ShapeDtypeStruct((t, n), F32),
                   jax.ShapeDtypeStruct((n, t), F32)],
        compiler_params=_cparams("parallel"),
        name="proj_dt",
    )(h, w, w.T)


def _proj_conv_kernel(h_ref, hp_ref, hn_ref, w_ref, cw_ref, cb_ref, xs_ref, b_out, bt_out, c_out, *ext_refs,
                      nsb):
    i = pl.program_id(0)
    bm = h_ref.shape[0]
    n = w_ref.shape[1]
    halo = SUBLANES
    pad = CONV_W // 2
    keep_prev = (i % nsb > 0).astype(F32)
    keep_next = (i % nsb < nsb - 1).astype(F32)
    nc = ext_refs[0].shape[1]
    for ext_ref, c in zip(ext_refs, range(0, n, nc)):
        w = w_ref[:, c:c + nc]
        ext_ref[halo:halo + bm, :] = _dot(h_ref[...], w)
        ext_ref[0:halo, :] = _dot(hp_ref[...], w)[BF16_ROWS - halo:, :] * keep_prev
        ext_ref[halo + bm:halo + bm + halo, :] = _dot(hn_ref[...], w)[0:halo, :] * keep_next
    rc = LANES
    for c in range(0, n, LANES):
        cs = slice(c, c + LANES)
        ext_ref = ext_refs[c // nc]
        es = slice(c % nc, c % nc + LANES)
        taps = [cw_ref[k:k + 1, cs] for k in range(CONV_W)]
        bias = cb_ref[:, cs]
        for r in range(0, bm, rc):
            base = halo - pad + r
            acc = bias + ext_ref[base:base + rc, es] * taps[0]
            for k in range(1, CONV_W):
                acc = acc + ext_ref[base + k:base + k + rc, es] * taps[k]
            y = _silu(acc)
            rs = slice(r, r + rc)
            if c < D_INNER:
                xs_ref[rs, cs] = y
            elif c < D_INNER + GN:
                cc = slice(c - D_INNER, c - D_INNER + LANES)
                b_out[rs, cc] = y.astype(b_out.dtype)
                bt_out[cc, rs] = y.T.astype(bt_out.dtype)
            else:
                cc = slice(c - D_INNER - GN, c - D_INNER - GN + LANES)
                c_out[rs, cc] = y.astype(c_out.dtype)


def _proj_conv(h, w, conv_w, conv_b, batch, bm=512):
    t, d = h.shape
    n = w.shape[1]
    s = t // batch
    nsb = s // bm
    hb = bm // BF16_ROWS
    last_hb = t // BF16_ROWS - 1
    nc = 2 * MXU_COLS
    return pl.pallas_call(
        functools.partial(_proj_conv_kernel, nsb=nsb),
        grid=(t // bm,),
        in_specs=[pl.BlockSpec((bm, d), lambda i: (i, 0)),
                  pl.BlockSpec((BF16_ROWS, d), lambda i: (jnp.maximum(i * hb - 1, 0), 0)),
                  pl.BlockSpec((BF16_ROWS, d), lambda i: (jnp.minimum((i + 1) * hb, last_hb), 0)),
                  pl.BlockSpec((d, n), lambda i: (0, 0)),
                  pl.BlockSpec((CONV_W, n), lambda i: (0, 0)),
                  pl.BlockSpec((1, n), lambda i: (0, 0))],
        out_specs=[pl.BlockSpec((bm, D_INNER), lambda i: (i, 0)),
                   pl.BlockSpec((bm, GN), lambda i: (i, 0)),
                   pl.BlockSpec((None, GN, bm), lambda i: (i // nsb, 0, i % nsb)),
                   pl.BlockSpec((bm, GN), lambda i: (i, 0))],
        out_shape=[jax.ShapeDtypeStruct((t, D_INNER), F32),
                   jax.ShapeDtypeStruct((t, GN), BF16),
                   jax.ShapeDtypeStruct((batch, GN, s), BF16),
                   jax.ShapeDtypeStruct((t, GN), BF16)],
        scratch_shapes=[pltpu.VMEM((bm + 2 * SUBLANES, nc), F32)] * (n // nc),
        compiler_params=_cparams("parallel"),
        name="proj_conv",
    )(h, h, h, w, conv_w, conv_b.reshape(1, n))


def _ssd_direction(direction, xs_ref, b_ref, bt_ref, c_ref, dt_ref, dtt_ref, bias_ref, biast_ref,
                   alog_ref, alogt_ref, dskip_ref, rep_ref, y_ref, state_ref):
    q = SSD_CHUNK
    hg = SSD_HEADS // SSD_GROUPS
    gw = hg * SSD_HEADDIM
    row = lax.broadcasted_iota(jnp.int32, (q, q), 0)
    col = lax.broadcasted_iota(jnp.int32, (q, q), 1)
    if direction == 0:
        keep = col <= row
        last = q - 1
    else:
        keep = col >= row
        last = 0
    tri = keep.astype(BF16)
    tri_t = (row <= col).astype(BF16) if direction == 0 else (row >= col).astype(BF16)

    dt = _softplus(dt_ref[...] + bias_ref[...])
    dtt = _softplus(dtt_ref[...] + biast_ref[...])
    dta = dt * (-jnp.exp(alog_ref[...]))
    dtat = dtt * (-jnp.exp(alogt_ref[...]))
    acs = _dot_exact_rhs(tri, dta)
    acst = _dot_exact_lhs(dtat, tri_t)

    rep3 = rep_ref[...]

    def expand(x):
        wide = jnp.concatenate([x, x, x, x], axis=1)
        hi, mid, lo = _split3(wide)
        lane = lax.broadcasted_iota(jnp.int32, wide.shape, 1)
        packed = jnp.where(lane < x.shape[1], hi, jnp.where(lane < 2 * x.shape[1], mid, lo))
        return _dot(packed, rep3)

    acs_e = expand(acs)
    dt_e = expand(dt)
    xs = xs_ref[...]
    xc = xs * dt_e
    xcb = xc.astype(BF16)
    last_e = acs_e[last:last + 1, :]
    xd = (xc * jnp.exp(last_e - acs_e)).astype(BF16)
    chunk_decay = jnp.exp(last_e)
    eacs = jnp.exp(acs_e)
    lane = lax.broadcasted_iota(jnp.int32, (q, LANES), 1)
    first_half = lane < SSD_HEADDIM

    for g in range(SSD_GROUPS):
        gs = slice(g * gw, (g + 1) * gw)
        ns = slice(g * SSD_STATE, (g + 1) * SSD_STATE)
        cg = c_ref[:, ns]
        bgt = bt_ref[ns, :]
        cb = _dot(cg, bgt)
        h_in = state_ref[:, gs]
        y_off = _dot(cg, h_in.astype(BF16)) * eacs[:, gs]
        st = _dot(bgt, xd[:, gs])
        state_ref[:, gs] = h_in * chunk_decay[:, gs] + st
        for jj in range(hg // 2):
            j = g * (hg // 2) + jj
            ms = []
            for hh in (2 * j, 2 * j + 1):
                ci = direction * SSD_HEADS + hh
                seg = acs[:, ci:ci + 1] - acst[ci:ci + 1, :]
                lm = jnp.exp(jnp.where(keep, seg, NEG_BIG))
                ms.append((cb * lm).astype(BF16))
            lhs = jnp.concatenate(ms, axis=1)
            x2 = xcb[:, j * LANES:(j + 1) * LANES]
            zero = jnp.zeros_like(x2)
            rhs = jnp.concatenate([jnp.where(first_half, x2, zero),
                                   jnp.where(first_half, zero, x2)], axis=0)
            y = _dot(lhs, rhs) + y_off[:, jj * LANES:(jj + 1) * LANES]
            if direction == 0:
                cs = slice(j * LANES, (j + 1) * LANES)
                y = y + xs[:, cs] * dskip_ref[:, cs]
            y_ref[:, j * LANES:(j + 1) * LANES] = y


def _ssd_kernel(xs_f, b_f, bt_f, c_f, dt_f, dtt_f, xs_b, b_b, bt_b, c_b, dt_b, dtt_b,
                bias_ref, biast_ref, alog_ref, alogt_ref, dskip_ref, rep_ref,
                yf_ref, yb_ref, sf_ref, sb_ref):
    @pl.when(pl.program_id(1) == 0)
    def _():
        sf_ref[...] = jnp.zeros_like(sf_ref)
        sb_ref[...] = jnp.zeros_like(sb_ref)

    _ssd_direction(0, xs_f, b_f, bt_f, c_f, dt_f, dtt_f, bias_ref, biast_ref, alog_ref, alogt_ref,
                   dskip_ref, rep_ref.at[0], yf_ref, sf_ref)
    _ssd_direction(1, xs_b, b_b, bt_b, c_b, dt_b, dtt_b, bias_ref, biast_ref, alog_ref, alogt_ref,
                   dskip_ref, rep_ref.at[1], yb_ref, sb_ref)


def _ssd_scan(xs, bm, bmt, cm, dt, dtt, dt_bias, a_log, d_skip):
    b, s, _ = xs.shape
    q = SSD_CHUNK
    nc = s // q
    nh2 = 2 * SSD_HEADS

    def fwd(bi, c):
        return c

    def bwd(bi, c):
        return nc - 1 - c

    def chunk_specs(cidx):
        return [pl.BlockSpec((None, q, D_INNER), lambda bi, c: (bi, cidx(bi, c), 0)),
                pl.BlockSpec((None, q, GN), lambda bi, c: (bi, cidx(bi, c), 0)),
                pl.BlockSpec((None, GN, q), lambda bi, c: (bi, 0, cidx(bi, c))),
                pl.BlockSpec((None, q, GN), lambda bi, c: (bi, cidx(bi, c), 0)),
                pl.BlockSpec((None, q, nh2), lambda bi, c: (bi, cidx(bi, c), 0)),
                pl.BlockSpec((nh2, q), lambda bi, c: (0, bi * nc + cidx(bi, c)))]

    def const(shape):
        return pl.BlockSpec(shape, lambda bi, c: (0,) * len(shape))

    rep = np.zeros((2, 4 * nh2, D_INNER), np.float32)
    for d in range(2):
        for h in range(SSD_HEADS):
            for piece in range(3):
                rep[d, piece * nh2 + d * SSD_HEADS + h, h * SSD_HEADDIM:(h + 1) * SSD_HEADDIM] = 1.0
    rep = jnp.asarray(rep, BF16)
    dskip_e = jnp.repeat(d_skip.astype(F32), SSD_HEADDIM).reshape(1, D_INNER)
    bias = dt_bias.reshape(1, nh2).astype(F32)
    alog = a_log.reshape(1, nh2).astype(F32)

    ins = (xs, bm, bmt, cm, dt.reshape(b, s, nh2), dtt)
    return pl.pallas_call(
        _ssd_kernel,
        grid=(b, nc),
        in_specs=chunk_specs(fwd) + chunk_specs(bwd) + [
            const((1, nh2)), const((nh2, 1)), const((1, nh2)), const((nh2, 1)),
            const((1, D_INNER)), const((2, 4 * nh2, D_INNER))],
        out_specs=[pl.BlockSpec((None, q, D_INNER), lambda bi, c: (bi, c, 0)),
                   pl.BlockSpec((None, q, D_INNER), lambda bi, c: (bi, nc - 1 - c, 0))],
        out_shape=[jax.ShapeDtypeStruct((b, s, D_INNER), F32),
                   jax.ShapeDtypeStruct((b, s, D_INNER), F32)],
        scratch_shapes=[pltpu.VMEM((SSD_STATE, D_INNER), F32),
                        pltpu.VMEM((SSD_STATE, D_INNER), F32)],
        compiler_params=_cparams("parallel", "arbitrary"),
        name="ssd_scan",
    )(*ins, *ins, bias, bias.reshape(nh2, 1), alog, alog.reshape(nh2, 1), dskip_e, rep)


def _flash_kernel(shift_ref, q_ref, k_ref, v_ref, o_ref, qa_ref, ka_ref, vat_ref, acc_ref, m_ref, l_ref,
                  *, bq, bk, bk_slow):
    seq = k_ref.shape[0]
    d = HEAD_DIM
    rows = GQA_REP * bq
    shift = shift_ref[0, 0]
    first_lane = lax.broadcasted_iota(jnp.int32, (1, d), 1) == 0

    @pl.when(pl.program_id(2) == 0)
    def _():
        one_col = jnp.where(first_lane, 1.0, 0.0).astype(BF16)
        first_row = lax.broadcasted_iota(jnp.int32, (VT_ROWS - d, 1), 0) == 0
        one_row = jnp.where(first_row, 1.0, 0.0).astype(BF16)
        qa_ref[:, d:2 * d] = jnp.broadcast_to(jnp.where(first_lane, -shift, 0.0).astype(BF16), (rows, d))

        def fill(c, carry):
            off = pl.multiple_of(c * bk_slow, bk_slow)
            ka_ref[pl.ds(off, bk_slow), 0:d] = k_ref[pl.ds(off, bk_slow), :]
            ka_ref[pl.ds(off, bk_slow), d:2 * d] = jnp.broadcast_to(one_col, (bk_slow, d))
            vat_ref[0:d, pl.ds(off, bk_slow)] = v_ref[pl.ds(off, bk_slow), :].astype(F32).T.astype(BF16)
            vat_ref[d:VT_ROWS, pl.ds(off, bk_slow)] = jnp.broadcast_to(one_row, (VT_ROWS - d, bk_slow))
            return carry

        lax.fori_loop(0, seq // bk_slow, fill, 0)

    for r in range(GQA_REP):
        qa_ref[r * bq:(r + 1) * bq, 0:d] = q_ref[:, r * d:(r + 1) * d]
    use_fast = shift * 2.0 <= FAST_SOFTMAX_LOG2_RANGE

    @pl.when(use_fast)
    def _():
        qa = qa_ref[...]
        acc_t = jnp.zeros((VT_ROWS, rows), F32)
        for u in range(seq // bk):
            ks = slice(u * bk, (u + 1) * bk)
            s_t = _dot_nt(ka_ref[ks, :], qa)
            acc_t = acc_t + _dot(vat_ref[:, ks], jnp.exp2(s_t).astype(BF16))
        out = (acc_t[0:d, :] * (1.0 / acc_t[d:d + 1, :])).T
        for r in range(GQA_REP):
            o_ref[:, r * d:(r + 1) * d] = out[r * bq:(r + 1) * bq, :].astype(o_ref.dtype)

    @pl.when(jnp.logical_not(use_fast))
    def _():
        m_ref[...] = jnp.full_like(m_ref, NEG_BIG)
        l_ref[...] = jnp.zeros_like(l_ref)
        acc_ref[...] = jnp.zeros_like(acc_ref)

        def body(j, carry):
            off = pl.multiple_of(j * bk_slow, bk_slow)
            s = _dot_nt(qa_ref[:, 0:d], k_ref[pl.ds(off, bk_slow), :])
            m_prev = m_ref[...]
            m_new = jnp.maximum(m_prev, jnp.max(s, axis=-1, keepdims=True))
            alpha = jnp.exp2(m_prev - m_new)
            p = jnp.exp2(s - m_new)
            l_ref[...] = alpha * l_ref[...] + jnp.sum(p, axis=-1, keepdims=True)
            acc_ref[...] = alpha * acc_ref[...] + _dot(p.astype(BF16), v_ref[pl.ds(off, bk_slow), :])
            m_ref[...] = m_new
            return carry

        lax.fori_loop(0, seq // bk_slow, body, 0)
        inv = 1.0 / l_ref[...]
        for r in range(GQA_REP):
            rs = slice(r * bq, (r + 1) * bq)
            o_ref[:, r * d:(r + 1) * d] = (acc_ref[rs, :] * inv[rs, :]).astype(o_ref.dtype)


def _flash_gqa(q, k, v, logit_bound, bq=256, bk=512, bk_slow=1024):
    b, s, _ = q.shape
    bk = min(bk, s)
    bk_slow = min(bk_slow, s)
    rows = GQA_REP * bq
    gw = GQA_REP * HEAD_DIM
    return pl.pallas_call(
        functools.partial(_flash_kernel, bq=bq, bk=bk, bk_slow=bk_slow),
        grid=(b, GQA_KV_HEADS, s // bq),
        in_specs=[pl.BlockSpec(memory_space=pltpu.SMEM),
                  pl.BlockSpec((None, bq, gw), lambda bi, kh, i: (bi, i, kh)),
                  pl.BlockSpec((None, s, HEAD_DIM), lambda bi, kh, i: (bi, 0, kh)),
                  pl.BlockSpec((None, s, HEAD_DIM), lambda bi, kh, i: (bi, 0, kh))],
        out_specs=pl.BlockSpec((None, bq, gw), lambda bi, kh, i: (bi, i, kh)),
        out_shape=jax.ShapeDtypeStruct((b, s, GQA_WIDTH), BF16),
        scratch_shapes=[pltpu.VMEM((rows, 2 * HEAD_DIM), BF16),
                        pltpu.VMEM((s, 2 * HEAD_DIM), BF16),
                        pltpu.VMEM((VT_ROWS, s), BF16),
                        pltpu.VMEM((rows, HEAD_DIM), F32),
                        pltpu.VMEM((rows, 1), F32),
                        pltpu.VMEM((rows, 1), F32)],
        compiler_params=_cparams("parallel", "parallel", "arbitrary"),
        name="flash_gqa",
    )(jnp.reshape(logit_bound, (1, 1)).astype(F32), q, k, v)


def _band_kernel(q_ref, kp_ref, kc_ref, kn_ref, vp_ref, vc_ref, vn_ref, bias_ref, o_ref, lse_ref,
                 *, dil, nsub, length):
    i = pl.program_id(1)
    tq = BAND_TILE
    col = lax.broadcasted_iota(jnp.int32, (tq, 2 * tq), 1)
    for u in range(nsub):
        kpos = (i * nsub + u) * tq - DIL_HALF + col
        inside = (kpos >= 0) & (kpos < length)
        rows = slice(u * tq, (u + 1) * tq)
        head = slice(u * tq - DIL_HALF, u * tq)
        tail = slice((u + 1) * tq, (u + 1) * tq + DIL_HALF)
        for r in range(dil):
            for hd in range(DIL_HEADS_PER_GROUP):
                c0 = (r * DIL_HEADS_PER_GROUP + hd) * HEAD_DIM
                sl = slice(c0, c0 + HEAD_DIM)

                def window(prev_ref, cur_ref, next_ref):
                    first = prev_ref[:, sl] if u == 0 else cur_ref[head, sl]
                    last = next_ref[:, sl] if u == nsub - 1 else cur_ref[tail, sl]
                    return jnp.concatenate([first, cur_ref[rows, sl], last], axis=0)

                s = _dot_nt(q_ref[rows, sl], window(kp_ref, kc_ref, kn_ref)) + bias_ref[hd]
                s = jnp.where(inside, s, NEG_BIG)
                m = jnp.max(s, axis=-1, keepdims=True)
                p = jnp.exp(s - m)
                l = jnp.sum(p, axis=-1, keepdims=True)
                o_ref[rows, sl] = _dot(p.astype(BF16), window(vp_ref, vc_ref, vn_ref)) / l
                lse_ref[rows, sl] = jnp.broadcast_to(m + jnp.log(l), (tq, HEAD_DIM))


def _band_attention(qr, kr, vr, bias_t, group, dil, batch):
    rows_total, width = qr.shape
    length = rows_total // batch
    nsub = max(1, BAND_TILES_PER_STEP // dil)
    bqr = nsub * BAND_TILE
    nb = length // bqr
    hb = bqr // DIL_HALF
    last_hb = length // DIL_HALF - 1
    view = lambda t: t.reshape(batch, length, width)

    cur = pl.BlockSpec((None, bqr, width), lambda bi, i: (bi, i, 0))
    prev = pl.BlockSpec((None, DIL_HALF, width), lambda bi, i: (bi, jnp.maximum(i * hb - 1, 0), 0))
    nxt = pl.BlockSpec((None, DIL_HALF, width), lambda bi, i: (bi, jnp.minimum((i + 1) * hb, last_hb), 0))
    o, lse = pl.pallas_call(
        functools.partial(_band_kernel, dil=dil, nsub=nsub, length=length),
        grid=(batch, nb),
        in_specs=[cur, prev, cur, nxt, prev, cur, nxt,
                  pl.BlockSpec((DIL_HEADS_PER_GROUP, BAND_TILE, 2 * BAND_TILE), lambda bi, i: (0, 0, 0))],
        out_specs=[cur, cur],
        out_shape=[jax.ShapeDtypeStruct((batch, length, width), F32),
                   jax.ShapeDtypeStruct((batch, length, width), F32)],
        compiler_params=_cparams("parallel", "parallel"),
        name=f"band_attn_g{group}",
    )(view(qr), view(kr), view(kr), view(kr), view(vr), view(vr), view(vr), bias_t)
    return o.reshape(rows_total, width), lse.reshape(rows_total, width)


def _t5_bucket(rel):
    nb = N_BUCKETS // 2
    max_exact = nb // 2
    ret = jnp.where(rel > 0, nb, 0)
    n = jnp.abs(rel)
    nf = jnp.maximum(n, 1).astype(F32)
    large = max_exact + (jnp.log(nf / max_exact) / math.log(REL_MAX_DIST / max_exact)
                         * (nb - max_exact)).astype(jnp.int32)
    large = jnp.minimum(large, nb - 1)
    return ret + jnp.where(n < max_exact, n, large)


def _band_bias_tiles(rel_bias):
    tq = BAND_TILE
    period = 3 * tq
    tiles = []
    for g, (_, dil) in enumerate(DIL_PATTERNS):
        dist = jnp.arange(-DIL_HALF, DIL_HALF + 1, dtype=jnp.int32) * dil
        tbl = rel_bias.astype(F32)[_t5_bucket(dist)]
        tbl = tbl[:, g * DIL_HEADS_PER_GROUP:(g + 1) * DIL_HEADS_PER_GROUP].T
        nh = tbl.shape[0]
        u = jnp.concatenate([jnp.full((nh, tq - 1), NEG_BIG, F32), tbl,
                             jnp.full((nh, period - tq - 2 * DIL_HALF), NEG_BIG, F32)], axis=1)
        rows = jnp.tile(u, (1, tq))[:, :tq * (period - 1)].reshape(nh, tq, period - 1)
        tiles.append(rows[:, :, tq - 1:3 * tq - 1])
    return tiles


def _to_token_order(src_ref, scr_ref, dil):
    if dil == 1:
        return src_ref[...]
    n = src_ref.shape[0]
    nlt = DIL_OUT // LANES
    for r in range(dil):
        for c in range(nlt):
            c0 = r * DIL_OUT + c * LANES
            scr_ref[c, pl.ds(r, n, stride=dil), :] = src_ref[:, c0:c0 + LANES]
    return jnp.concatenate([scr_ref[c] for c in range(nlt)], axis=1)


def _merge_kernel(x_ref, yf_ref, yb_ref, zs_ref, gssd_ref, ygqa_ref,
                  o0_ref, l0_ref, o1_ref, l1_ref, o2_ref, l2_ref, gates_ref,
                  wssd_ref, wgqa_ref, wdil_ref, wout_ref, gpost_ref, o_ref, *scr):
    y = (yf_ref[...] + yb_ref[...]) * zs_ref[...]
    y_ssd = _rms(y, gssd_ref[...]).astype(BF16)
    dils = [dil for _, dil in DIL_PATTERNS]
    o0, o1, o2 = (_to_token_order(r, s, dl) for r, s, dl in zip((o0_ref, o1_ref, o2_ref), scr[0:3], dils))
    l0, l1, l2 = (_to_token_order(r, s, dl) for r, s, dl in zip((l0_ref, l1_ref, l2_ref), scr[3:6], dils))
    mx = jnp.maximum(jnp.maximum(l0, l1), l2)
    e0, e1, e2 = jnp.exp(l0 - mx), jnp.exp(l1 - mx), jnp.exp(l2 - mx)
    y_dil = ((e0 * o0 + e1 * o1 + e2 * o2) / (e0 + e1 + e2)).astype(BF16)
    d = D_MODEL
    mix = gates_ref[:, 0:d] * _dot(y_ssd, wssd_ref[...])
    mix = mix + gates_ref[:, d:2 * d] * _dot(ygqa_ref[...], wgqa_ref[...])
    mix = mix + gates_ref[:, 2 * d:3 * d] * _dot(y_dil, wdil_ref[...])
    out = _dot(mix.astype(BF16), wout_ref[...])
    o_ref[...] = x_ref[...] + _rms(out, gpost_ref[...])


def _merge(x2d, yf, yb, zs, g_ssd, y_gqa, dil_parts, gates, w_ssd, w_gqa, w_dil, w_out, g_post, bm=256):
    t, d = x2d.shape
    row = lambda n: pl.BlockSpec((bm, n), lambda i: (i, 0))
    const = lambda a: pl.BlockSpec(a.shape, lambda i: (0, 0))
    (o0, l0), (o1, l1), (o2, l2) = dil_parts
    g_ssd = g_ssd.reshape(1, -1)
    g_post = g_post.reshape(1, -1)
    args = (x2d, yf, yb, zs, g_ssd, y_gqa, o0, l0, o1, l1, o2, l2, gates, w_ssd, w_gqa, w_dil, w_out, g_post)
    specs = [row(d), row(D_INNER), row(D_INNER), row(D_INNER), const(g_ssd), row(GQA_WIDTH)]
    for _, dil in DIL_PATTERNS:
        specs += [pl.BlockSpec((bm // dil, dil * DIL_OUT), lambda i: (i, 0))] * 2
    specs += [row(N_BRANCHES * d), const(w_ssd), const(w_gqa), const(w_dil), const(w_out), const(g_post)]
    return pl.pallas_call(
        _merge_kernel,
        grid=(t // bm,),
        in_specs=specs,
        out_specs=row(d),
        out_shape=jax.ShapeDtypeStruct((t, d), F32),
        scratch_shapes=[pltpu.VMEM((DIL_OUT // LANES, bm, LANES), F32)] * 6,
        compiler_params=_cparams("parallel"),
        name="merge_out",
    )(*args)


def _ffn_kernel(x_ref, pa_ref, pb_ref, gpre_ref, wg_ref, wu_ref, wd_ref, gpost_ref, gple_ref, wpg_ref, wp_ref,
                gnext_ref, *o_refs, nc, na, split_output):
    x = x_ref[...]
    h = _rms(x, gpre_ref[...]).astype(BF16)
    ff = jnp.zeros(x.shape, F32)
    for c in range(0, wg_ref.shape[1], nc):
        a = _silu(_dot(h, wg_ref[:, c:c + nc])) * _dot(h, wu_ref[:, c:c + nc])
        ff = ff + _dot(a.astype(BF16), wd_ref[c:c + nc, :])
    x = x + _rms(ff, gpost_ref[...])
    hn = _rms(x, gple_ref[...]).astype(BF16)
    gate = _sigmoid(_dot(hn, wpg_ref[...]))

    in_first = pl.program_id(0) < na
    p = jnp.where(in_first, pa_ref[...], pb_ref[...])
    out = x + _dot(p.astype(BF16), wp_ref[...]) * gate
    if split_output:
        def write_first():
            o_refs[0][...] = out

        pl.when(in_first)(write_first)
        o_refs[1][...] = out
    else:
        o_refs[0][...] = out
        o_refs[1][...] = _rms(out, gnext_ref[...]).astype(o_refs[1].dtype)


def _ffn_ple(x2d, pa, pb, layer, g_pre, w_gate, w_up, w_down, g_post, g_ple, w_ple_gate, w_ple, g_next,
             split_output, bm=512):
    t, d = x2d.shape
    ta, tb = pa.shape[1], pb.shape[1]
    na = ta // bm
    row = lambda n: pl.BlockSpec((bm, n), lambda i: (i, 0))
    first = lambda i: jnp.minimum(i, na - 1)
    second = lambda i: jnp.maximum(i - na, 0)
    const = lambda a: pl.BlockSpec(a.shape, lambda i: (0, 0), pipeline_mode=pl.Buffered(1))
    g_pre, g_post, g_ple, g_next = (g.reshape(1, d) for g in (g_pre, g_post, g_ple, g_next))
    consts = (g_pre, w_gate, w_up, w_down, g_post, g_ple, w_ple_gate, w_ple, g_next)
    specs = [row(d),
             pl.BlockSpec((None, bm, PLE_DIM), lambda i: (layer, first(i), 0)),
             pl.BlockSpec((None, bm, PLE_DIM), lambda i: (layer, second(i), 0))]
    specs += [const(a) for a in consts]
    if split_output:
        out_specs = [pl.BlockSpec((bm, d), lambda i: (first(i), 0)),
                     pl.BlockSpec((bm, d), lambda i: (second(i), 0))]
        out_shape = [jax.ShapeDtypeStruct((ta, d), F32), jax.ShapeDtypeStruct((tb, d), F32)]
    else:
        out_specs = [row(d), row(d)]
        out_shape = [jax.ShapeDtypeStruct((t, d), F32), jax.ShapeDtypeStruct((t, d), BF16)]
    nc = 256 if FFN_DIM % 256 == 0 else FFN_DIM
    return pl.pallas_call(
        functools.partial(_ffn_kernel, nc=nc, na=na, split_output=split_output),
        grid=(t // bm,),
        in_specs=specs,
        out_specs=out_specs,
        out_shape=out_shape,
        compiler_params=_cparams("arbitrary"),
        name="ffn_ple",
    )(x2d, pa, pb, *consts)


def _rope_tables(seq):
    pos = jnp.arange(seq)
    row = (pos // GRID_W).astype(F32)
    colp = (pos % GRID_W).astype(F32)
    n_pairs = HEAD_DIM // 4
    inv = ROPE_THETA ** (-jnp.arange(n_pairs, dtype=F32) / n_pairs)
    ang = jnp.concatenate([row[:, None] * inv, colp[:, None] * inv], axis=-1)
    c, s = jnp.cos(ang), jnp.sin(ang)
    return jnp.concatenate([c, c], axis=-1), jnp.concatenate([-s, s], axis=-1)


def _layer(h, x2d, b, s, pa, pb, i, P, cosf, sinf, bias_tiles, last):
    t = b * s
    offs = np.concatenate([[0], np.cumsum(IN_WIDTHS)])
    w_in = P["w_in"][i]
    wz, wxbc, wdt, wq, wk, wv, wdq, wdk, wdv, wgates = (
        w_in[:, offs[j]:offs[j + 1]] for j in range(len(IN_WIDTHS)))
    def deinterleave(w):
        lead = w.shape[:-1]
        return w.reshape(*lead, -1, HEAD_DIM // 2, 2).swapaxes(-1, -2).reshape(*lead, -1)

    wq, wk = deinterleave(wq), deinterleave(wk)
    bf = lambda w: w.astype(BF16)

    zs = _proj(h, bf(wz), BF16, act="silu", name="proj_z")
    xs, bm, bmt, cm = _proj_conv(h, bf(wxbc), P["conv_w"][i], P["conv_b"][i], b)
    dt, dtt = _proj_dt(h, bf(wdt))
    q = _proj_qk(h, bf(wq), deinterleave(P["g_q"][i]), cosf, sinf, ATTN_SCALE * LOG2E, s, name="proj_q")
    k = _proj_qk(h, bf(wk), deinterleave(P["g_k"][i]), cosf, sinf, 1.0, s, name="proj_k")
    v = _proj(h, bf(wv), BF16, name="proj_v")
    dil_qkv = _proj_dil(h, bf(jnp.concatenate([wdq, wdk, wdv], axis=1)))
    gates = _proj(h, bf(wgates), BF16, act="sigmoid", name="proj_gates")

    yf, yb = _ssd_scan(xs.reshape(b, s, D_INNER), bm.reshape(b, s, GN), bmt, cm.reshape(b, s, GN), dt, dtt,
                       P["dt_bias"][i], P["a_log"][i], P["d_skip"][i])

    logit_bound = (HEAD_DIM * ATTN_SCALE * LOG2E * NORM_SLACK
                   * jnp.max(jnp.abs(P["g_q"][i])) * jnp.max(jnp.abs(P["g_k"][i])))
    y_gqa = _flash_gqa(q.reshape(b, s, GQA_WIDTH), k.reshape(b, s, GQA_KV_WIDTH),
                       v.reshape(b, s, GQA_KV_WIDTH), logit_bound)

    dil_parts = []
    ng = len(DIL_PATTERNS)
    for g, (_, dil) in enumerate(DIL_PATTERNS):
        dil_parts.append(_band_attention(dil_qkv[g], dil_qkv[ng + g], dil_qkv[2 * ng + g],
                                         bias_tiles[g], g, dil, b))

    x2d = _merge(x2d, yf.reshape(t, D_INNER), yb.reshape(t, D_INNER), zs, P["g_ssd"][i],
                 y_gqa.reshape(t, GQA_WIDTH), dil_parts, gates,
                 bf(P["w_br_ssd"][i]), bf(P["w_br_gqa"][i]), bf(P["w_br_dil"][i]), bf(P["w_out"][i]),
                 P["g_post_mix"][i])
    g_next = P["g_pre_mix"][i if last else i + 1]
    return _ffn_ple(x2d, pa, pb, i, P["g_pre_ffn"][i], bf(P["w_gate"][i]), bf(P["w_up"][i]),
                    bf(P["w_down"][i]), P["g_post_ffn"][i], P["g_ple"][i], bf(P["w_ple_gate"][i]),
                    bf(P["w_ple"][i]), g_next, split_output=last)


def kernel(x_prompt, x_sample, p_prompt, p_sample, w_in, conv_w, conv_b, dt_bias, a_log, d_skip, g_ssd, g_q, g_k, w_br_ssd, w_br_gqa, w_br_dil, w_out, g_pre_mix, g_post_mix, g_pre_ffn, g_post_ffn, w_gate, w_up, w_down, w_ple, g_ple, w_ple_gate, rel_bias):
    P = dict(w_in=w_in, conv_w=conv_w, conv_b=conv_b, dt_bias=dt_bias, a_log=a_log, d_skip=d_skip,
             g_ssd=g_ssd, g_q=g_q, g_k=g_k, w_br_ssd=w_br_ssd, w_br_gqa=w_br_gqa, w_br_dil=w_br_dil,
             w_out=w_out, g_pre_mix=g_pre_mix, g_post_mix=g_post_mix, g_pre_ffn=g_pre_ffn,
             g_post_ffn=g_post_ffn, w_gate=w_gate, w_up=w_up, w_down=w_down, w_ple=w_ple,
             g_ple=g_ple, w_ple_gate=w_ple_gate)
    ba, s, d = x_prompt.shape
    bb = x_sample.shape[0]
    assert x_sample.shape[1:] == (s, d), "both request groups must have the same sequence length"
    depth = p_prompt.shape[0]
    pa = p_prompt.reshape(depth, ba * s, PLE_DIM)
    pb = p_sample.reshape(depth, bb * s, PLE_DIM)
    cosf, sinf = _rope_tables(s)
    bias_tiles = _band_bias_tiles(rel_bias)
    h, x2d = _prenorm_cat(x_prompt.reshape(ba * s, d), x_sample.reshape(bb * s, d), g_pre_mix[0])
    for i in range(depth - 1):
        x2d, h = _layer(h, x2d, ba + bb, s, pa, pb, i, P, cosf, sinf, bias_tiles, last=False)
    ya, yb = _layer(h, x2d, ba + bb, s, pa, pb, depth - 1, P, cosf, sinf, bias_tiles, last=True)
    return (ya.reshape(ba, s, d), yb.reshape(bb, s, d))
```

```python
import functools
import math

import jax
import jax.numpy as jnp
import numpy as np
from jax import lax
from jax.experimental import pallas as pl
from jax.experimental.pallas import tpu as pltpu

F32 = jnp.float32
BF16 = jnp.bfloat16

D_MODEL = 1024
GRID_W = 64
PLE_DIM = 256
EPS = 1e-6
SSD_HEADS = 32
SSD_HEADDIM = 64
D_INNER = SSD_HEADS * SSD_HEADDIM
SSD_GROUPS = 4
SSD_STATE = 128
SSD_CHUNK = 128
CONV_W = 5
GN = SSD_GROUPS * SSD_STATE
XBC_DIM = D_INNER + 2 * GN
HEAD_DIM = 128
GQA_Q_HEADS = 16
GQA_KV_HEADS = 4
GQA_REP = GQA_Q_HEADS // GQA_KV_HEADS
GQA_WIDTH = GQA_Q_HEADS * HEAD_DIM
GQA_KV_WIDTH = GQA_KV_HEADS * HEAD_DIM
ROPE_THETA = 10000.0
ATTN_SCALE = HEAD_DIM ** -0.5
DIL_PATTERNS = ((128, 1), (512, 4), (2048, 16))
DIL_HEADS_PER_GROUP = 4
DIL_HEADS = len(DIL_PATTERNS) * DIL_HEADS_PER_GROUP
DIL_WIDTH = DIL_HEADS * HEAD_DIM
DIL_OUT = DIL_HEADS_PER_GROUP * HEAD_DIM
DIL_HALF = 64
BAND_TILE = 2 * DIL_HALF
BAND_TILES_PER_STEP = 16
N_BUCKETS = 32
REL_MAX_DIST = 2048
FFN_DIM = ((8 * D_MODEL // 3 + 255) // 256) * 256
N_BRANCHES = 3
IN_WIDTHS = (D_INNER, XBC_DIM, 2 * SSD_HEADS, GQA_WIDTH, GQA_KV_WIDTH, GQA_KV_WIDTH,
             DIL_WIDTH, DIL_WIDTH, DIL_WIDTH, N_BRANCHES * D_MODEL)

LANES = 128
SUBLANES = 8
BF16_ROWS = 16
MXU_COLS = 256
VMEM_LIMIT = 56 * 1024 * 1024
NEG_BIG = -1e30
LOG2E = math.log2(math.e)
VT_ROWS = HEAD_DIM + BF16_ROWS
NORM_SLACK = 1.02
FAST_SOFTMAX_LOG2_RANGE = 100.0


def _cparams(*sem):
    return pltpu.CompilerParams(dimension_semantics=sem, vmem_limit_bytes=VMEM_LIMIT)


def _rms(x, g):
    ms = jnp.mean(x * x, axis=-1, keepdims=True)
    return x * lax.rsqrt(ms + EPS) * g


def _sigmoid(x):
    return 1.0 / (1.0 + jnp.exp(-x))


def _silu(x):
    return x * _sigmoid(x)


def _softplus(x):
    return jnp.maximum(x, 0.0) + jnp.log1p(jnp.exp(-jnp.abs(x)))


def _dot(a, b):
    return jnp.dot(a, b, preferred_element_type=F32)


def _dot_nt(a, b):
    return lax.dot_general(a, b, (((1,), (1,)), ((), ())), preferred_element_type=F32)


def _split3(x):
    hi = x.astype(BF16)
    r1 = x - hi.astype(F32)
    mid = r1.astype(BF16)
    lo = (r1 - mid.astype(F32)).astype(BF16)
    return hi, mid, lo


def _dot_exact_lhs(x, m_bf16, pieces=3):
    parts = _split3(x)[:pieces]
    acc = _dot(parts[0], m_bf16)
    for p in parts[1:]:
        acc = acc + _dot(p, m_bf16)
    return acc


def _dot_exact_rhs(m_bf16, x, pieces=3):
    parts = _split3(x)[:pieces]
    acc = _dot(m_bf16, parts[0])
    for p in parts[1:]:
        acc = acc + _dot(m_bf16, p)
    return acc


def _prenorm_cat_kernel(xa_ref, xb_ref, g_ref, h_ref, x_ref, *, na):
    def emit(src_ref):
        x = src_ref[...]
        h_ref[...] = _rms(x, g_ref[...]).astype(h_ref.dtype)
        x_ref[...] = x

    pl.when(pl.program_id(0) < na)(lambda: emit(xa_ref))
    pl.when(pl.program_id(0) >= na)(lambda: emit(xb_ref))


def _prenorm_cat(xa, xb, g, bm=1024):
    (ta, d), tb = xa.shape, xb.shape[0]
    na = ta // bm
    t = ta + tb
    return pl.pallas_call(
        functools.partial(_prenorm_cat_kernel, na=na),
        grid=(t // bm,),
        in_specs=[pl.BlockSpec((bm, d), lambda i: (jnp.minimum(i, na - 1), 0)),
                  pl.BlockSpec((bm, d), lambda i: (jnp.maximum(i - na, 0), 0)),
                  pl.BlockSpec((1, d), lambda i: (0, 0))],
        out_specs=[pl.BlockSpec((bm, d), lambda i: (i, 0)),
                   pl.BlockSpec((bm, d), lambda i: (i, 0))],
        out_shape=[jax.ShapeDtypeStruct((t, d), BF16),
                   jax.ShapeDtypeStruct((t, d), F32)],
        compiler_params=_cparams("arbitrary"),
        name="prenorm_cat",
    )(xa, xb, g.reshape(1, d))


def _proj_kernel(h_ref, w_ref, o_ref, *, act, nc):
    h = h_ref[...]
    n = o_ref.shape[-1]
    for c in range(0, n, nc):
        y = _dot(h, w_ref[:, c:c + nc])
        if act == "silu":
            y = _silu(y)
        elif act == "sigmoid":
            y = _sigmoid(y)
        o_ref[:, c:c + nc] = y.astype(o_ref.dtype)


def _proj(h, w, out_dtype, act=None, bm=1024, name="proj"):
    t, d = h.shape
    n = w.shape[1]
    nc = 512 if n % 512 == 0 else n
    return pl.pallas_call(
        functools.partial(_proj_kernel, act=act, nc=nc),
        grid=(t // bm,),
        in_specs=[pl.BlockSpec((bm, d), lambda i: (i, 0)),
                  pl.BlockSpec((d, n), lambda i: (0, 0))],
        out_specs=pl.BlockSpec((bm, n), lambda i: (i, 0)),
        out_shape=jax.ShapeDtypeStruct((t, n), out_dtype),
        compiler_params=_cparams("parallel"),
        name=name,
    )(h, w)


def _proj_qk_kernel(h_ref, w_ref, g_ref, cos_ref, sin_ref, o_ref, *, scale):
    h = h_ref[...]
    g = g_ref[...]
    cosf = cos_ref[...]
    sinf = sin_ref[...]
    for hd in range(o_ref.shape[-1] // HEAD_DIM):
        sl = slice(hd * HEAD_DIM, (hd + 1) * HEAD_DIM)
        y = _rms(_dot(h, w_ref[:, sl]), g)
        y = y * cosf + pltpu.roll(y, HEAD_DIM // 2, 1) * sinf
        if scale != 1.0:
            y = y * scale
        o_ref[:, sl] = y.astype(o_ref.dtype)


def _proj_qk(h, w, g, cosf, sinf, scale, seq, bm=512, name="proj_qk"):
    t, d = h.shape
    n = w.shape[1]
    nsb = seq // bm
    return pl.pallas_call(
        functools.partial(_proj_qk_kernel, scale=scale),
        grid=(t // bm,),
        in_specs=[pl.BlockSpec((bm, d), lambda i: (i, 0)),
                  pl.BlockSpec((d, n), lambda i: (0, 0)),
                  pl.BlockSpec((1, HEAD_DIM), lambda i: (0, 0)),
                  pl.BlockSpec((bm, HEAD_DIM), lambda i: (i % nsb, 0)),
                  pl.BlockSpec((bm, HEAD_DIM), lambda i: (i % nsb, 0))],
        out_specs=pl.BlockSpec((bm, n), lambda i: (i, 0)),
        out_shape=jax.ShapeDtypeStruct((t, n), BF16),
        compiler_params=_cparams("parallel"),
        name=name,
    )(h, w, g.reshape(1, HEAD_DIM), cosf, sinf)


def _proj_dil_kernel(h_ref, w_ref, *refs):
    outs, scr = refs[:-1], refs[-1]
    h = h_ref[...]
    bm = h.shape[0]
    nlt = DIL_OUT // LANES
    for t in range(3):
        for g, (_, dil) in enumerate(DIL_PATTERNS):
            o_ref = outs[t * len(DIL_PATTERNS) + g]
            c0 = (t * len(DIL_PATTERNS) + g) * DIL_OUT
            y = _dot(h, w_ref[:, c0:c0 + DIL_OUT])
            if t == 0:
                y = y * ATTN_SCALE
            if dil == 1:
                o_ref[...] = y.astype(o_ref.dtype)
                continue
            for c in range(nlt):
                scr[c] = y[:, c * LANES:(c + 1) * LANES]
            for r in range(dil):
                for c in range(nlt):
                    o0 = r * DIL_OUT + c * LANES
                    o_ref[:, o0:o0 + LANES] = scr[c, pl.ds(r, bm // dil, stride=dil), :].astype(o_ref.dtype)


def _proj_dil(h, w, bm=1024):
    t, d = h.shape
    n = w.shape[1]
    out_specs, out_shapes = [], []
    for _ in range(3):
        for _, dil in DIL_PATTERNS:
            out_specs.append(pl.BlockSpec((bm // dil, dil * DIL_OUT), lambda i: (i, 0)))
            out_shapes.append(jax.ShapeDtypeStruct((t // dil, dil * DIL_OUT), BF16))
    return pl.pallas_call(
        _proj_dil_kernel,
        grid=(t // bm,),
        in_specs=[pl.BlockSpec((bm, d), lambda i: (i, 0)),
                  pl.BlockSpec((d, n), lambda i: (0, 0))],
        out_specs=out_specs,
        out_shape=out_shapes,
        scratch_shapes=[pltpu.VMEM((DIL_OUT // LANES, bm, LANES), F32)],
        compiler_params=_cparams("parallel"),
        name="proj_dil",
    )(h, w)


def _proj_dt_kernel(h_ref, w_ref, wt_ref, o_ref, ot_ref):
    h = h_ref[...]
    o_ref[...] = _dot(h, w_ref[...])
    ot_ref[...] = _dot_nt(wt_ref[...], h)


def _proj_dt(h, w, bm=512):
    t, d = h.shape
    n = w.shape[1]
    return pl.pallas_call(
        _proj_dt_kernel,
        grid=(t // bm,),
        in_specs=[pl.BlockSpec((bm, d), lambda i: (i, 0)),
                  pl.BlockSpec((d, n), lambda i: (0, 0)),
                  pl.BlockSpec((n, d), lambda i: (0, 0))],
        out_specs=[pl.BlockSpec((bm, n), lambda i: (i, 0)),
                   pl.BlockSpec((n, bm), lambda i: (0, i))],
        out_shape=[jax.ShapeDtypeStruct((t, n), F32),
                   jax.ShapeDtypeStruct((n, t), F32)],
        compiler_params=_cparams("parallel"),
        name="proj_dt",
    )(h, w, w.T)


def _proj_conv_kernel(h_ref, hp_ref, hn_ref, w_ref, cw_ref, cb_ref, xs_ref, b_out, bt_out, c_out, *ext_refs,
                      nsb):
    i = pl.program_id(0)
    bm = h_ref.shape[0]
    n = w_ref.shape[1]
    halo = SUBLANES
    pad = CONV_W // 2
    keep_prev = (i % nsb > 0).astype(F32)
    keep_next = (i % nsb < nsb - 1).astype(F32)
    nc = ext_refs[0].shape[1]
    for ext_ref, c in zip(ext_refs, range(0, n, nc)):
        w = w_ref[:, c:c + nc]
        ext_ref[halo:halo + bm, :] = _dot(h_ref[...], w)
        ext_ref[0:halo, :] = _dot(hp_ref[...], w)[BF16_ROWS - halo:, :] * keep_prev
        ext_ref[halo + bm:halo + bm + halo, :] = _dot(hn_ref[...], w)[0:halo, :] * keep_next
    rc = LANES
    for c in range(0, n, LANES):
        cs = slice(c, c + LANES)
        ext_ref = ext_refs[c // nc]
        es = slice(c % nc, c % nc + LANES)
        taps = [cw_ref[k:k + 1, cs] for k in range(CONV_W)]
        bias = cb_ref[:, cs]
        for r in range(0, bm, rc):
            n_blk = rc + 2 * halo
            blk = ext_ref[r:r + n_blk, es]
            acc = bias + blk[halo:halo + rc, :] * taps[pad]
            for k in range(CONV_W):
                if k != pad:
                    acc = acc + pltpu.roll(blk, n_blk - (halo - pad + k), 0)[0:rc, :] * taps[k]
            y = _silu(acc)
            rs = slice(r, r + rc)
            if c < D_INNER:
                xs_ref[rs, cs] = y
            elif c < D_INNER + GN:
                cc = slice(c - D_INNER, c - D_INNER + LANES)
                b_out[rs, cc] = y.astype(b_out.dtype)
                bt_out[cc, rs] = y.T.astype(bt_out.dtype)
            else:
                cc = slice(c - D_INNER - GN, c - D_INNER - GN + LANES)
                c_out[rs, cc] = y.astype(c_out.dtype)


def _proj_conv(h, w, conv_w, conv_b, batch, bm=1024):
    t, d = h.shape
    n = w.shape[1]
    s = t // batch
    nsb = s // bm
    hb = bm // BF16_ROWS
    last_hb = t // BF16_ROWS - 1
    nc = 2 * MXU_COLS
    return pl.pallas_call(
        functools.partial(_proj_conv_kernel, nsb=nsb),
        grid=(t // bm,),
        in_specs=[pl.BlockSpec((bm, d), lambda i: (i, 0)),
                  pl.BlockSpec((BF16_ROWS, d), lambda i: (jnp.maximum(i * hb - 1, 0), 0)),
                  pl.BlockSpec((BF16_ROWS, d), lambda i: (jnp.minimum((i + 1) * hb, last_hb), 0)),
                  pl.BlockSpec((d, n), lambda i: (0, 0)),
                  pl.BlockSpec((CONV_W, n), lambda i: (0, 0)),
                  pl.BlockSpec((1, n), lambda i: (0, 0))],
        out_specs=[pl.BlockSpec((bm, D_INNER), lambda i: (i, 0)),
                   pl.BlockSpec((bm, GN), lambda i: (i, 0)),
                   pl.BlockSpec((None, GN, bm), lambda i: (i // nsb, 0, i % nsb)),
                   pl.BlockSpec((bm, GN), lambda i: (i, 0))],
        out_shape=[jax.ShapeDtypeStruct((t, D_INNER), F32),
                   jax.ShapeDtypeStruct((t, GN), BF16),
                   jax.ShapeDtypeStruct((batch, GN, s), BF16),
                   jax.ShapeDtypeStruct((t, GN), BF16)],
        scratch_shapes=[pltpu.VMEM((bm + 2 * SUBLANES, nc), F32)] * (n // nc),
        compiler_params=_cparams("parallel"),
        name="proj_conv",
    )(h, h, h, w, conv_w, conv_b.reshape(1, n))


def _ssd_direction(direction, xs_ref, b_ref, bt_ref, c_ref, dt_ref, dtt_ref, bias_ref, biast_ref,
                   alog_ref, alogt_ref, dskip_ref, rep_ref, y_ref, state_ref):
    q = SSD_CHUNK
    hg = SSD_HEADS // SSD_GROUPS
    gw = hg * SSD_HEADDIM
    row = lax.broadcasted_iota(jnp.int32, (q, q), 0)
    col = lax.broadcasted_iota(jnp.int32, (q, q), 1)
    if direction == 0:
        keep = col <= row
        last = q - 1
    else:
        keep = col >= row
        last = 0
    tri = keep.astype(BF16)
    tri_t = (row <= col).astype(BF16) if direction == 0 else (row >= col).astype(BF16)

    dt = _softplus(dt_ref[...] + bias_ref[...])
    dtt = _softplus(dtt_ref[...] + biast_ref[...])
    dta = dt * (-LOG2E * jnp.exp(alog_ref[...]))
    dtat = dtt * (-LOG2E * jnp.exp(alogt_ref[...]))
    acs = _dot_exact_rhs(tri, dta)
    acst = _dot_exact_lhs(dtat, tri_t)

    def pack(x):
        wide = jnp.concatenate([x, x, x, x], axis=1)
        hi, mid, lo = _split3(wide)
        lane = lax.broadcasted_iota(jnp.int32, wide.shape, 1)
        return jnp.where(lane < x.shape[1], hi, jnp.where(lane < 2 * x.shape[1], mid, lo))

    acs_p = pack(acs)
    dt_p = pack(dt)
    lane = lax.broadcasted_iota(jnp.int32, (q, LANES), 1)
    first_half = lane < SSD_HEADDIM

    for g in range(SSD_GROUPS):
        gs = slice(g * gw, (g + 1) * gw)
        ns = slice(g * SSD_STATE, (g + 1) * SSD_STATE)
        acs_e = _dot(acs_p, rep_ref[:, gs])
        dt_e = _dot(dt_p, rep_ref[:, gs])
        xs = xs_ref[:, gs]
        xc = xs * dt_e
        xcb = xc.astype(BF16)
        last_e = acs_e[last:last + 1, :]
        xd = (xc * jnp.exp2(last_e - acs_e)).astype(BF16)
        cg = c_ref[:, ns]
        bgt = bt_ref[ns, :]
        cb = _dot(cg, bgt)
        h_in = state_ref[:, gs]
        y_off = _dot(cg, h_in.astype(BF16)) * jnp.exp2(acs_e)
        st = _dot(bgt, xd)
        state_ref[:, gs] = h_in * jnp.exp2(last_e) + st
        for jj in range(hg // 2):
            j = g * (hg // 2) + jj
            ms = []
            for hh in (2 * j, 2 * j + 1):
                ci = direction * SSD_HEADS + hh
                seg = acs[:, ci:ci + 1] - acst[ci:ci + 1, :]
                lm = jnp.exp2(jnp.where(keep, seg, NEG_BIG))
                ms.append((cb * lm).astype(BF16))
            lhs = jnp.concatenate(ms, axis=1)
            x2 = xcb[:, jj * LANES:(jj + 1) * LANES]
            zero = jnp.zeros_like(x2)
            rhs = jnp.concatenate([jnp.where(first_half, x2, zero),
                                   jnp.where(first_half, zero, x2)], axis=0)
            y = _dot(lhs, rhs) + y_off[:, jj * LANES:(jj + 1) * LANES]
            if direction == 0:
                cs = slice(j * LANES, (j + 1) * LANES)
                y = y + xs[:, jj * LANES:(jj + 1) * LANES] * dskip_ref[:, cs]
            y_ref[:, j * LANES:(j + 1) * LANES] = y


def _ssd_kernel(xs_f, b_f, bt_f, c_f, dt_f, dtt_f, xs_b, b_b, bt_b, c_b, dt_b, dtt_b,
                bias_ref, biast_ref, alog_ref, alogt_ref, dskip_ref, rep_ref,
                yf_ref, yb_ref, sf_ref, sb_ref):
    @pl.when(pl.program_id(1) == 0)
    def _():
        sf_ref[...] = jnp.zeros_like(sf_ref)
        sb_ref[...] = jnp.zeros_like(sb_ref)

    _ssd_direction(0, xs_f, b_f, bt_f, c_f, dt_f, dtt_f, bias_ref, biast_ref, alog_ref, alogt_ref,
                   dskip_ref, rep_ref.at[0], yf_ref, sf_ref)
    _ssd_direction(1, xs_b, b_b, bt_b, c_b, dt_b, dtt_b, bias_ref, biast_ref, alog_ref, alogt_ref,
                   dskip_ref, rep_ref.at[1], yb_ref, sb_ref)


def _ssd_scan(xs, bm, bmt, cm, dt, dtt, dt_bias, a_log, d_skip):
    b, s, _ = xs.shape
    q = SSD_CHUNK
    nc = s // q
    nh2 = 2 * SSD_HEADS

    def fwd(bi, c):
        return c

    def bwd(bi, c):
        return nc - 1 - c

    def chunk_specs(cidx):
        return [pl.BlockSpec((None, q, D_INNER), lambda bi, c: (bi, cidx(bi, c), 0)),
                pl.BlockSpec((None, q, GN), lambda bi, c: (bi, cidx(bi, c), 0)),
                pl.BlockSpec((None, GN, q), lambda bi, c: (bi, 0, cidx(bi, c))),
                pl.BlockSpec((None, q, GN), lambda bi, c: (bi, cidx(bi, c), 0)),
                pl.BlockSpec((None, q, nh2), lambda bi, c: (bi, cidx(bi, c), 0)),
                pl.BlockSpec((nh2, q), lambda bi, c: (0, bi * nc + cidx(bi, c)))]

    def const(shape):
        return pl.BlockSpec(shape, lambda bi, c: (0,) * len(shape))

    rep = np.zeros((2, 4 * nh2, D_INNER), np.float32)
    for d in range(2):
        for h in range(SSD_HEADS):
            for piece in range(3):
                rep[d, piece * nh2 + d * SSD_HEADS + h, h * SSD_HEADDIM:(h + 1) * SSD_HEADDIM] = 1.0
    rep = jnp.asarray(rep, BF16)
    dskip_e = jnp.repeat(d_skip.astype(F32), SSD_HEADDIM).reshape(1, D_INNER)
    bias = dt_bias.reshape(1, nh2).astype(F32)
    alog = a_log.reshape(1, nh2).astype(F32)

    ins = (xs, bm, bmt, cm, dt.reshape(b, s, nh2), dtt)
    return pl.pallas_call(
        _ssd_kernel,
        grid=(b, nc),
        in_specs=chunk_specs(fwd) + chunk_specs(bwd) + [
            const((1, nh2)), const((nh2, 1)), const((1, nh2)), const((nh2, 1)),
            const((1, D_INNER)), const((2, 4 * nh2, D_INNER))],
        out_specs=[pl.BlockSpec((None, q, D_INNER), lambda bi, c: (bi, c, 0)),
                   pl.BlockSpec((None, q, D_INNER), lambda bi, c: (bi, nc - 1 - c, 0))],
        out_shape=[jax.ShapeDtypeStruct((b, s, D_INNER), F32),
                   jax.ShapeDtypeStruct((b, s, D_INNER), F32)],
        scratch_shapes=[pltpu.VMEM((SSD_STATE, D_INNER), F32),
                        pltpu.VMEM((SSD_STATE, D_INNER), F32)],
        compiler_params=_cparams("parallel", "arbitrary"),
        name="ssd_scan",
    )(*ins, *ins, bias, bias.reshape(nh2, 1), alog, alog.reshape(nh2, 1), dskip_e, rep)


def _flash_kernel(shift_ref, q_ref, k_ref, v_ref, o_ref, qa_ref, ka_ref, vat_ref, acc_ref, m_ref, l_ref,
                  *, bq, bk, bk_slow):
    seq = k_ref.shape[0]
    d = HEAD_DIM
    rows = GQA_REP * bq
    shift = shift_ref[0, 0]
    first_lane = lax.broadcasted_iota(jnp.int32, (1, d), 1) == 0

    @pl.when(pl.program_id(2) == 0)
    def _():
        one_col = jnp.where(first_lane, 1.0, 0.0).astype(BF16)
        first_row = lax.broadcasted_iota(jnp.int32, (VT_ROWS - d, 1), 0) == 0
        one_row = jnp.where(first_row, 1.0, 0.0).astype(BF16)
        qa_ref[:, d:2 * d] = jnp.broadcast_to(jnp.where(first_lane, -shift, 0.0).astype(BF16), (rows, d))

        def fill(c, carry):
            off = pl.multiple_of(c * bk_slow, bk_slow)
            ka_ref[pl.ds(off, bk_slow), 0:d] = k_ref[pl.ds(off, bk_slow), :]
            ka_ref[pl.ds(off, bk_slow), d:2 * d] = jnp.broadcast_to(one_col, (bk_slow, d))
            vat_ref[0:d, pl.ds(off, bk_slow)] = v_ref[pl.ds(off, bk_slow), :].astype(F32).T.astype(BF16)
            vat_ref[d:VT_ROWS, pl.ds(off, bk_slow)] = jnp.broadcast_to(one_row, (VT_ROWS - d, bk_slow))
            return carry

        lax.fori_loop(0, seq // bk_slow, fill, 0)

    for r in range(GQA_REP):
        qa_ref[r * bq:(r + 1) * bq, 0:d] = q_ref[:, r * d:(r + 1) * d]
    use_fast = shift * 2.0 <= FAST_SOFTMAX_LOG2_RANGE

    @pl.when(use_fast)
    def _():
        qa = qa_ref[...]
        acc_t = jnp.zeros((VT_ROWS, rows), F32)
        for u in range(seq // bk):
            ks = slice(u * bk, (u + 1) * bk)
            s_t = _dot_nt(ka_ref[ks, :], qa)
            acc_t = acc_t + _dot(vat_ref[:, ks], jnp.exp2(s_t).astype(BF16))
        out = (acc_t[0:d, :] * (1.0 / acc_t[d:d + 1, :])).T
        for r in range(GQA_REP):
            o_ref[:, r * d:(r + 1) * d] = out[r * bq:(r + 1) * bq, :].astype(o_ref.dtype)

    @pl.when(jnp.logical_not(use_fast))
    def _():
        m_ref[...] = jnp.full_like(m_ref, NEG_BIG)
        l_ref[...] = jnp.zeros_like(l_ref)
        acc_ref[...] = jnp.zeros_like(acc_ref)

        def body(j, carry):
            off = pl.multiple_of(j * bk_slow, bk_slow)
            s = _dot_nt(qa_ref[:, 0:d], k_ref[pl.ds(off, bk_slow), :])
            m_prev = m_ref[...]
            m_new = jnp.maximum(m_prev, jnp.max(s, axis=-1, keepdims=True))
            alpha = jnp.exp2(m_prev - m_new)
            p = jnp.exp2(s - m_new)
            l_ref[...] = alpha * l_ref[...] + jnp.sum(p, axis=-1, keepdims=True)
            acc_ref[...] = alpha * acc_ref[...] + _dot(p.astype(BF16), v_ref[pl.ds(off, bk_slow), :])
            m_ref[...] = m_new
            return carry

        lax.fori_loop(0, seq // bk_slow, body, 0)
        inv = 1.0 / l_ref[...]
        for r in range(GQA_REP):
            rs = slice(r * bq, (r + 1) * bq)
            o_ref[:, r * d:(r + 1) * d] = (acc_ref[rs, :] * inv[rs, :]).astype(o_ref.dtype)


def _flash_gqa(q, k, v, logit_bound, bq=512, bk=1024, bk_slow=1024):
    b, s, _ = q.shape
    bk = min(bk, s)
    bk_slow = min(bk_slow, s)
    rows = GQA_REP * bq
    gw = GQA_REP * HEAD_DIM
    return pl.pallas_call(
        functools.partial(_flash_kernel, bq=bq, bk=bk, bk_slow=bk_slow),
        grid=(b, GQA_KV_HEADS, s // bq),
        in_specs=[pl.BlockSpec(memory_space=pltpu.SMEM),
                  pl.BlockSpec((None, bq, gw), lambda bi, kh, i: (bi, i, kh)),
                  pl.BlockSpec((None, s, HEAD_DIM), lambda bi, kh, i: (bi, 0, kh)),
                  pl.BlockSpec((None, s, HEAD_DIM), lambda bi, kh, i: (bi, 0, kh))],
        out_specs=pl.BlockSpec((None, bq, gw), lambda bi, kh, i: (bi, i, kh)),
        out_shape=jax.ShapeDtypeStruct((b, s, GQA_WIDTH), BF16),
        scratch_shapes=[pltpu.VMEM((rows, 2 * HEAD_DIM), BF16),
                        pltpu.VMEM((s, 2 * HEAD_DIM), BF16),
                        pltpu.VMEM((VT_ROWS, s), BF16),
                        pltpu.VMEM((rows, HEAD_DIM), F32),
                        pltpu.VMEM((rows, 1), F32),
                        pltpu.VMEM((rows, 1), F32)],
        compiler_params=_cparams("parallel", "parallel", "arbitrary"),
        name="flash_gqa",
    )(jnp.reshape(logit_bound, (1, 1)).astype(F32), q, k, v)


def _band_kernel(q_ref, kp_ref, kc_ref, kn_ref, vp_ref, vc_ref, vn_ref, bias_ref, o_ref, lse_ref,
                 *, dil, nsub, length):
    i = pl.program_id(1)
    tq = BAND_TILE
    col = lax.broadcasted_iota(jnp.int32, (tq, 2 * tq), 1)
    for u in range(nsub):
        kpos = (i * nsub + u) * tq - DIL_HALF + col
        inside = (kpos >= 0) & (kpos < length)
        rows = slice(u * tq, (u + 1) * tq)
        head = slice(u * tq - DIL_HALF, u * tq)
        tail = slice((u + 1) * tq, (u + 1) * tq + DIL_HALF)
        for r in range(dil):
            for hd in range(DIL_HEADS_PER_GROUP):
                c0 = (r * DIL_HEADS_PER_GROUP + hd) * HEAD_DIM
                sl = slice(c0, c0 + HEAD_DIM)

                def window(prev_ref, cur_ref, next_ref):
                    first = prev_ref[:, sl] if u == 0 else cur_ref[head, sl]
                    last = next_ref[:, sl] if u == nsub - 1 else cur_ref[tail, sl]
                    return jnp.concatenate([first, cur_ref[rows, sl], last], axis=0)

                s = _dot_nt(q_ref[rows, sl], window(kp_ref, kc_ref, kn_ref)) + bias_ref[hd]
                s = jnp.where(inside, s, NEG_BIG)
                m = jnp.max(s, axis=-1, keepdims=True)
                p = jnp.exp(s - m)
                l = jnp.sum(p, axis=-1, keepdims=True)
                o_ref[rows, sl] = _dot(p.astype(BF16), window(vp_ref, vc_ref, vn_ref)) / l
                lse_ref[rows, sl] = jnp.broadcast_to(m + jnp.log(l), (tq, HEAD_DIM))


def _band_attention(qr, kr, vr, bias_t, group, dil, batch):
    rows_total, width = qr.shape
    length = rows_total // batch
    nsub = max(1, BAND_TILES_PER_STEP // dil)
    bqr = nsub * BAND_TILE
    nb = length // bqr
    hb = bqr // DIL_HALF
    last_hb = length // DIL_HALF - 1
    view = lambda t: t.reshape(batch, length, width)

    cur = pl.BlockSpec((None, bqr, width), lambda bi, i: (bi, i, 0))
    prev = pl.BlockSpec((None, DIL_HALF, width), lambda bi, i: (bi, jnp.maximum(i * hb - 1, 0), 0))
    nxt = pl.BlockSpec((None, DIL_HALF, width), lambda bi, i: (bi, jnp.minimum((i + 1) * hb, last_hb), 0))
    o, lse = pl.pallas_call(
        functools.partial(_band_kernel, dil=dil, nsub=nsub, length=length),
        grid=(batch, nb),
        in_specs=[cur, prev, cur, nxt, prev, cur, nxt,
                  pl.BlockSpec((DIL_HEADS_PER_GROUP, BAND_TILE, 2 * BAND_TILE), lambda bi, i: (0, 0, 0))],
        out_specs=[cur, cur],
        out_shape=[jax.ShapeDtypeStruct((batch, length, width), F32),
                   jax.ShapeDtypeStruct((batch, length, width), F32)],
        compiler_params=_cparams("parallel", "parallel"),
        name=f"band_attn_g{group}",
    )(view(qr), view(kr), view(kr), view(kr), view(vr), view(vr), view(vr), bias_t)
    return o.reshape(rows_total, width), lse.reshape(rows_total, width)


def _t5_bucket(rel):
    nb = N_BUCKETS // 2
    max_exact = nb // 2
    ret = jnp.where(rel > 0, nb, 0)
    n = jnp.abs(rel)
    nf = jnp.maximum(n, 1).astype(F32)
    large = max_exact + (jnp.log(nf / max_exact) / math.log(REL_MAX_DIST / max_exact)
                         * (nb - max_exact)).astype(jnp.int32)
    large = jnp.minimum(large, nb - 1)
    return ret + jnp.where(n < max_exact, n, large)


def _band_bias_tiles(rel_bias):
    tq = BAND_TILE
    period = 3 * tq
    tiles = []
    for g, (_, dil) in enumerate(DIL_PATTERNS):
        dist = jnp.arange(-DIL_HALF, DIL_HALF + 1, dtype=jnp.int32) * dil
        tbl = rel_bias.astype(F32)[_t5_bucket(dist)]
        tbl = tbl[:, g * DIL_HEADS_PER_GROUP:(g + 1) * DIL_HEADS_PER_GROUP].T
        nh = tbl.shape[0]
        u = jnp.concatenate([jnp.full((nh, tq - 1), NEG_BIG, F32), tbl,
                             jnp.full((nh, period - tq - 2 * DIL_HALF), NEG_BIG, F32)], axis=1)
        rows = jnp.tile(u, (1, tq))[:, :tq * (period - 1)].reshape(nh, tq, period - 1)
        tiles.append(rows[:, :, tq - 1:3 * tq - 1])
    return tiles


def _to_token_order(src_ref, scr_ref, dil):
    if dil == 1:
        return src_ref[...]
    n = src_ref.shape[0]
    nlt = DIL_OUT // LANES
    for r in range(dil):
        for c in range(nlt):
            c0 = r * DIL_OUT + c * LANES
            scr_ref[c, pl.ds(r, n, stride=dil), :] = src_ref[:, c0:c0 + LANES]
    return jnp.concatenate([scr_ref[c] for c in range(nlt)], axis=1)


def _merge_kernel(x_ref, yf_ref, yb_ref, zs_ref, gssd_ref, ygqa_ref,
                  o0_ref, l0_ref, o1_ref, l1_ref, o2_ref, l2_ref, gates_ref,
                  wssd_ref, wgqa_ref, wdil_ref, wout_ref, gpost_ref, o_ref, *scr):
    y = (yf_ref[...] + yb_ref[...]) * zs_ref[...]
    y_ssd = _rms(y, gssd_ref[...]).astype(BF16)
    dils = [dil for _, dil in DIL_PATTERNS]
    o0, o1, o2 = (_to_token_order(r, s, dl) for r, s, dl in zip((o0_ref, o1_ref, o2_ref), scr[0:3], dils))
    l0, l1, l2 = (_to_token_order(r, s, dl) for r, s, dl in zip((l0_ref, l1_ref, l2_ref), scr[3:6], dils))
    mx = jnp.maximum(jnp.maximum(l0, l1), l2)
    e0, e1, e2 = jnp.exp(l0 - mx), jnp.exp(l1 - mx), jnp.exp(l2 - mx)
    y_dil = ((e0 * o0 + e1 * o1 + e2 * o2) / (e0 + e1 + e2)).astype(BF16)
    d = D_MODEL
    mix = gates_ref[:, 0:d] * _dot(y_ssd, wssd_ref[...])
    mix = mix + gates_ref[:, d:2 * d] * _dot(ygqa_ref[...], wgqa_ref[...])
    mix = mix + gates_ref[:, 2 * d:3 * d] * _dot(y_dil, wdil_ref[...])
    out = _dot(mix.astype(BF16), wout_ref[...])
    o_ref[...] = x_ref[...] + _rms(out, gpost_ref[...])


def _merge(x2d, yf, yb, zs, g_ssd, y_gqa, dil_parts, gates, w_ssd, w_gqa, w_dil, w_out, g_post, bm=256):
    t, d = x2d.shape
    row = lambda n: pl.BlockSpec((bm, n), lambda i: (i, 0))
    const = lambda a: pl.BlockSpec(a.shape, lambda i: (0, 0))
    (o0, l0), (o1, l1), (o2, l2) = dil_parts
    g_ssd = g_ssd.reshape(1, -1)
    g_post = g_post.reshape(1, -1)
    args = (x2d, yf, yb, zs, g_ssd, y_gqa, o0, l0, o1, l1, o2, l2, gates, w_ssd, w_gqa, w_dil, w_out, g_post)
    specs = [row(d), row(D_INNER), row(D_INNER), row(D_INNER), const(g_ssd), row(GQA_WIDTH)]
    for _, dil in DIL_PATTERNS:
        specs += [pl.BlockSpec((bm // dil, dil * DIL_OUT), lambda i: (i, 0))] * 2
    specs += [row(N_BRANCHES * d), const(w_ssd), const(w_gqa), const(w_dil), const(w_out), const(g_post)]
    return pl.pallas_call(
        _merge_kernel,
        grid=(t // bm,),
        in_specs=specs,
        out_specs=row(d),
        out_shape=jax.ShapeDtypeStruct((t, d), F32),
        scratch_shapes=[pltpu.VMEM((DIL_OUT // LANES, bm, LANES), F32)] * 6,
        compiler_params=_cparams("parallel"),
        name="merge_out",
    )(*args)


def _ffn_kernel(x_ref, pa_ref, pb_ref, gpre_ref, wg_ref, wu_ref, wd_ref, gpost_ref, gple_ref, wpg_ref, wp_ref,
                gnext_ref, *o_refs, nc, na, split_output):
    x = x_ref[...]
    h = _rms(x, gpre_ref[...]).astype(BF16)
    ff = jnp.zeros(x.shape, F32)
    for c in range(0, wg_ref.shape[1], nc):
        a = _silu(_dot(h, wg_ref[:, c:c + nc])) * _dot(h, wu_ref[:, c:c + nc])
        ff = ff + _dot(a.astype(BF16), wd_ref[c:c + nc, :])
    x = x + _rms(ff, gpost_ref[...])
    hn = _rms(x, gple_ref[...]).astype(BF16)
    gate = _sigmoid(_dot(hn, wpg_ref[...]))

    in_first = pl.program_id(0) < na
    p = jnp.where(in_first, pa_ref[...], pb_ref[...])
    out = x + _dot(p.astype(BF16), wp_ref[...]) * gate
    if split_output:
        def write_first():
            o_refs[0][...] = out

        pl.when(in_first)(write_first)
        o_refs[1][...] = out
    else:
        o_refs[0][...] = out
        o_refs[1][...] = _rms(out, gnext_ref[...]).astype(o_refs[1].dtype)


def _ffn_ple(x2d, pa, pb, layer, g_pre, w_gate, w_up, w_down, g_post, g_ple, w_ple_gate, w_ple, g_next,
             split_output, bm=512):
    t, d = x2d.shape
    ta, tb = pa.shape[1], pb.shape[1]
    na = ta // bm
    row = lambda n: pl.BlockSpec((bm, n), lambda i: (i, 0))
    first = lambda i: jnp.minimum(i, na - 1)
    second = lambda i: jnp.maximum(i - na, 0)
    const = lambda a: pl.BlockSpec(a.shape, lambda i: (0, 0), pipeline_mode=pl.Buffered(1))
    g_pre, g_post, g_ple, g_next = (g.reshape(1, d) for g in (g_pre, g_post, g_ple, g_next))
    consts = (g_pre, w_gate, w_up, w_down, g_post, g_ple, w_ple_gate, w_ple, g_next)
    specs = [row(d),
             pl.BlockSpec((None, bm, PLE_DIM), lambda i: (layer, first(i), 0)),
             pl.BlockSpec((None, bm, PLE_DIM), lambda i: (layer, second(i), 0))]
    specs += [const(a) for a in consts]
    if split_output:
        out_specs = [pl.BlockSpec((bm, d), lambda i: (first(i), 0)),
                     pl.BlockSpec((bm, d), lambda i: (second(i), 0))]
        out_shape = [jax.ShapeDtypeStruct((ta, d), F32), jax.ShapeDtypeStruct((tb, d), F32)]
    else:
        out_specs = [row(d), row(d)]
        out_shape = [jax.ShapeDtypeStruct((t, d), F32), jax.ShapeDtypeStruct((t, d), BF16)]
    nc = 256 if FFN_DIM % 256 == 0 else FFN_DIM
    return pl.pallas_call(
        functools.partial(_ffn_kernel, nc=nc, na=na, split_output=split_output),
        grid=(t // bm,),
        in_specs=specs,
        out_specs=out_specs,
        out_shape=out_shape,
        compiler_params=_cparams("arbitrary"),
        name="ffn_ple",
    )(x2d, pa, pb, *consts)


def _rope_tables(seq):
    pos = jnp.arange(seq)
    row = (pos // GRID_W).astype(F32)
    colp = (pos % GRID_W).astype(F32)
    n_pairs = HEAD_DIM // 4
    inv = ROPE_THETA ** (-jnp.arange(n_pairs, dtype=F32) / n_pairs)
    ang = jnp.concatenate([row[:, None] * inv, colp[:, None] * inv], axis=-1)
    c, s = jnp.cos(ang), jnp.sin(ang)
    return jnp.concatenate([c, c], axis=-1), jnp.concatenate([-s, s], axis=-1)


def _layer(h, x2d, b, s, pa, pb, i, P, cosf, sinf, bias_tiles, last):
    t = b * s
    offs = np.concatenate([[0], np.cumsum(IN_WIDTHS)])
    w_in = P["w_in"][i]
    wz, wxbc, wdt, wq, wk, wv, wdq, wdk, wdv, wgates = (
        w_in[:, offs[j]:offs[j + 1]] for j in range(len(IN_WIDTHS)))
    def deinterleave(w):
        lead = w.shape[:-1]
        return w.reshape(*lead, -1, HEAD_DIM // 2, 2).swapaxes(-1, -2).reshape(*lead, -1)

    wq, wk = deinterleave(wq), deinterleave(wk)
    bf = lambda w: w.astype(BF16)

    zs = _proj(h, bf(wz), BF16, act="silu", name="proj_z")
    xs, bm, bmt, cm = _proj_conv(h, bf(wxbc), P["conv_w"][i], P["conv_b"][i], b)
    dt, dtt = _proj_dt(h, bf(wdt))
    q = _proj_qk(h, bf(wq), deinterleave(P["g_q"][i]), cosf, sinf, ATTN_SCALE * LOG2E, s, name="proj_q")
    k = _proj_qk(h, bf(wk), deinterleave(P["g_k"][i]), cosf, sinf, 1.0, s, name="proj_k")
    v = _proj(h, bf(wv), BF16, name="proj_v")
    dil_qkv = _proj_dil(h, bf(jnp.concatenate([wdq, wdk, wdv], axis=1)))
    gates = _proj(h, bf(wgates), BF16, act="sigmoid", name="proj_gates")

    yf, yb = _ssd_scan(xs.reshape(b, s, D_INNER), bm.reshape(b, s, GN), bmt, cm.reshape(b, s, GN), dt, dtt,
                       P["dt_bias"][i], P["a_log"][i], P["d_skip"][i])

    logit_bound = (HEAD_DIM * ATTN_SCALE * LOG2E * NORM_SLACK
                   * jnp.max(jnp.abs(P["g_q"][i])) * jnp.max(jnp.abs(P["g_k"][i])))
    y_gqa = _flash_gqa(q.reshape(b, s, GQA_WIDTH), k.reshape(b, s, GQA_KV_WIDTH),
                       v.reshape(b, s, GQA_KV_WIDTH), logit_bound)

    dil_parts = []
    ng = len(DIL_PATTERNS)
    for g, (_, dil) in enumerate(DIL_PATTERNS):
        dil_parts.append(_band_attention(dil_qkv[g], dil_qkv[ng + g], dil_qkv[2 * ng + g],
                                         bias_tiles[g], g, dil, b))

    x2d = _merge(x2d, yf.reshape(t, D_INNER), yb.reshape(t, D_INNER), zs, P["g_ssd"][i],
                 y_gqa.reshape(t, GQA_WIDTH), dil_parts, gates,
                 bf(P["w_br_ssd"][i]), bf(P["w_br_gqa"][i]), bf(P["w_br_dil"][i]), bf(P["w_out"][i]),
                 P["g_post_mix"][i])
    g_next = P["g_pre_mix"][i if last else i + 1]
    return _ffn_ple(x2d, pa, pb, i, P["g_pre_ffn"][i], bf(P["w_gate"][i]), bf(P["w_up"][i]),
                    bf(P["w_down"][i]), P["g_post_ffn"][i], P["g_ple"][i], bf(P["w_ple_gate"][i]),
                    bf(P["w_ple"][i]), g_next, split_output=last)


def kernel(x_prompt, x_sample, p_prompt, p_sample, w_in, conv_w, conv_b, dt_bias, a_log, d_skip, g_ssd, g_q, g_k, w_br_ssd, w_br_gqa, w_br_dil, w_out, g_pre_mix, g_post_mix, g_pre_ffn, g_post_ffn, w_gate, w_up, w_down, w_ple, g_ple, w_ple_gate, rel_bias):
    P = dict(w_in=w_in, conv_w=conv_w, conv_b=conv_b, dt_bias=dt_bias, a_log=a_log, d_skip=d_skip,
             g_ssd=g_ssd, g_q=g_q, g_k=g_k, w_br_ssd=w_br_ssd, w_br_gqa=w_br_gqa, w_br_dil=w_br_dil,
             w_out=w_out, g_pre_mix=g_pre_mix, g_post_mix=g_post_mix, g_pre_ffn=g_pre_ffn,
             g_post_ffn=g_post_ffn, w_gate=w_gate, w_up=w_up, w_down=w_down, w_ple=w_ple,
             g_ple=g_ple, w_ple_gate=w_ple_gate)
    ba, s, d = x_prompt.shape
    bb = x_sample.shape[0]
    assert x_sample.shape[1:] == (s, d), "both request groups must have the same sequence length"
    depth = p_prompt.shape[0]
    pa = p_prompt.reshape(depth, ba * s, PLE_DIM)
    pb = p_sample.reshape(depth, bb * s, PLE_DIM)
    cosf, sinf = _rope_tables(s)
    bias_tiles = _band_bias_tiles(rel_bias)
    h, x2d = _prenorm_cat(x_prompt.reshape(ba * s, d), x_sample.reshape(bb * s, d), g_pre_mix[0])
    for i in range(depth - 1):
        x2d, h = _layer(h, x2d, ba + bb, s, pa, pb, i, P, cosf, sinf, bias_tiles, last=False)
    ya, yb = _layer(h, x2d, ba + bb, s, pa, pb, depth - 1, P, cosf, sinf, bias_tiles, last=True)
    return (ya.reshape(ba, s, d), yb.reshape(bb, s, d))
```

```python
import functools
import math

import jax
import jax.numpy as jnp
import numpy as np
from jax import lax
from jax.experimental import pallas as pl
from jax.experimental.pallas import tpu as pltpu

F32 = jnp.float32
BF16 = jnp.bfloat16

D_MODEL = 1024
GRID_W = 64
PLE_DIM = 256
EPS = 1e-6
SSD_HEADS = 32
SSD_HEADDIM = 64
D_INNER = SSD_HEADS * SSD_HEADDIM
SSD_GROUPS = 4
SSD_STATE = 128
SSD_CHUNK = 128
SSD_CHUNKS_PER_STEP = 4
CONV_W = 5
GN = SSD_GROUPS * SSD_STATE
XBC_DIM = D_INNER + 2 * GN
HEAD_DIM = 128
GQA_Q_HEADS = 16
GQA_KV_HEADS = 4
GQA_REP = GQA_Q_HEADS // GQA_KV_HEADS
GQA_WIDTH = GQA_Q_HEADS * HEAD_DIM
GQA_KV_WIDTH = GQA_KV_HEADS * HEAD_DIM
ROPE_THETA = 10000.0
ATTN_SCALE = HEAD_DIM ** -0.5
DIL_PATTERNS = ((128, 1), (512, 4), (2048, 16))
DIL_HEADS_PER_GROUP = 4
DIL_HEADS = len(DIL_PATTERNS) * DIL_HEADS_PER_GROUP
DIL_WIDTH = DIL_HEADS * HEAD_DIM
DIL_OUT = DIL_HEADS_PER_GROUP * HEAD_DIM
DIL_HALF = 64
BAND_TILE = 2 * DIL_HALF
BAND_TILES_PER_STEP = 16
N_BUCKETS = 32
REL_MAX_DIST = 2048
FFN_DIM = ((8 * D_MODEL // 3 + 255) // 256) * 256
N_BRANCHES = 3
IN_WIDTHS = (D_INNER, XBC_DIM, 2 * SSD_HEADS, GQA_WIDTH, GQA_KV_WIDTH, GQA_KV_WIDTH,
             DIL_WIDTH, DIL_WIDTH, DIL_WIDTH, N_BRANCHES * D_MODEL)

LANES = 128
SUBLANES = 8
BF16_ROWS = 16
MXU_COLS = 256
VMEM_LIMIT = 56 * 1024 * 1024
NEG_BIG = -1e30
LOG2E = math.log2(math.e)
VT_ROWS = HEAD_DIM + BF16_ROWS
NORM_SLACK = 1.02
FAST_SOFTMAX_LOG2_RANGE = 100.0


def _cparams(*sem):
    return pltpu.CompilerParams(dimension_semantics=sem, vmem_limit_bytes=VMEM_LIMIT)


def _rms(x, g):
    ms = jnp.mean(x * x, axis=-1, keepdims=True)
    return x * lax.rsqrt(ms + EPS) * g


def _sigmoid(x):
    return 1.0 / (1.0 + jnp.exp(-x))


def _silu(x):
    return x * _sigmoid(x)


def _softplus(x):
    return jnp.maximum(x, 0.0) + jnp.log1p(jnp.exp(-jnp.abs(x)))


def _dot(a, b):
    return jnp.dot(a, b, preferred_element_type=F32)


def _dot_nt(a, b):
    return lax.dot_general(a, b, (((1,), (1,)), ((), ())), preferred_element_type=F32)


def _split3(x):
    hi = x.astype(BF16)
    r1 = x - hi.astype(F32)
    mid = r1.astype(BF16)
    lo = (r1 - mid.astype(F32)).astype(BF16)
    return hi, mid, lo


def _dot_exact_lhs(x, m_bf16, pieces=3):
    parts = _split3(x)[:pieces]
    acc = _dot(parts[0], m_bf16)
    for p in parts[1:]:
        acc = acc + _dot(p, m_bf16)
    return acc


def _dot_exact_rhs(m_bf16, x, pieces=3):
    parts = _split3(x)[:pieces]
    acc = _dot(m_bf16, parts[0])
    for p in parts[1:]:
        acc = acc + _dot(m_bf16, p)
    return acc


def _prenorm_cat_kernel(xa_ref, xb_ref, g_ref, h_ref, x_ref, *, na):
    def emit(src_ref):
        x = src_ref[...]
        h_ref[...] = _rms(x, g_ref[...]).astype(h_ref.dtype)
        x_ref[...] = x

    pl.when(pl.program_id(0) < na)(lambda: emit(xa_ref))
    pl.when(pl.program_id(0) >= na)(lambda: emit(xb_ref))


def _prenorm_cat(xa, xb, g, bm=1024):
    (ta, d), tb = xa.shape, xb.shape[0]
    na = ta // bm
    t = ta + tb
    return pl.pallas_call(
        functools.partial(_prenorm_cat_kernel, na=na),
        grid=(t // bm,),
        in_specs=[pl.BlockSpec((bm, d), lambda i: (jnp.minimum(i, na - 1), 0)),
                  pl.BlockSpec((bm, d), lambda i: (jnp.maximum(i - na, 0), 0)),
                  pl.BlockSpec((1, d), lambda i: (0, 0))],
        out_specs=[pl.BlockSpec((bm, d), lambda i: (i, 0)),
                   pl.BlockSpec((bm, d), lambda i: (i, 0))],
        out_shape=[jax.ShapeDtypeStruct((t, d), BF16),
                   jax.ShapeDtypeStruct((t, d), F32)],
        compiler_params=_cparams("arbitrary"),
        name="prenorm_cat",
    )(xa, xb, g.reshape(1, d))


def _proj_kernel(h_ref, w_ref, o_ref, *, act, nc):
    h = h_ref[...]
    n = o_ref.shape[-1]
    for c in range(0, n, nc):
        y = _dot(h, w_ref[:, c:c + nc])
        if act == "silu":
            y = _silu(y)
        elif act == "sigmoid":
            y = _sigmoid(y)
        o_ref[:, c:c + nc] = y.astype(o_ref.dtype)


def _proj(h, w, out_dtype, act=None, bm=1024, name="proj"):
    t, d = h.shape
    n = w.shape[1]
    nc = 512 if n % 512 == 0 else n
    return pl.pallas_call(
        functools.partial(_proj_kernel, act=act, nc=nc),
        grid=(t // bm,),
        in_specs=[pl.BlockSpec((bm, d), lambda i: (i, 0)),
                  pl.BlockSpec((d, n), lambda i: (0, 0))],
        out_specs=pl.BlockSpec((bm, n), lambda i: (i, 0)),
        out_shape=jax.ShapeDtypeStruct((t, n), out_dtype),
        compiler_params=_cparams("parallel"),
        name=name,
    )(h, w)


def _proj_qk_kernel(h_ref, w_ref, g_ref, cos_ref, sin_ref, o_ref, *, scale):
    h = h_ref[...]
    g = g_ref[...]
    cosf = cos_ref[...]
    sinf = sin_ref[...]
    for hd in range(o_ref.shape[-1] // HEAD_DIM):
        sl = slice(hd * HEAD_DIM, (hd + 1) * HEAD_DIM)
        y = _rms(_dot(h, w_ref[:, sl]), g)
        y = y * cosf + pltpu.roll(y, HEAD_DIM // 2, 1) * sinf
        if scale != 1.0:
            y = y * scale
        o_ref[:, sl] = y.astype(o_ref.dtype)


def _proj_qk(h, w, g, cosf, sinf, scale, seq, bm=512, name="proj_qk"):
    t, d = h.shape
    n = w.shape[1]
    nsb = seq // bm
    return pl.pallas_call(
        functools.partial(_proj_qk_kernel, scale=scale),
        grid=(t // bm,),
        in_specs=[pl.BlockSpec((bm, d), lambda i: (i, 0)),
                  pl.BlockSpec((d, n), lambda i: (0, 0)),
                  pl.BlockSpec((1, HEAD_DIM), lambda i: (0, 0)),
                  pl.BlockSpec((bm, HEAD_DIM), lambda i: (i % nsb, 0)),
                  pl.BlockSpec((bm, HEAD_DIM), lambda i: (i % nsb, 0))],
        out_specs=pl.BlockSpec((bm, n), lambda i: (i, 0)),
        out_shape=jax.ShapeDtypeStruct((t, n), BF16),
        compiler_params=_cparams("parallel"),
        name=name,
    )(h, w, g.reshape(1, HEAD_DIM), cosf, sinf)


def _proj_dil_kernel(h_ref, w_ref, *refs):
    outs, scr = refs[:-1], refs[-1]
    h = h_ref[...]
    bm = h.shape[0]
    nlt = DIL_OUT // LANES
    for t in range(3):
        for g, (_, dil) in enumerate(DIL_PATTERNS):
            o_ref = outs[t * len(DIL_PATTERNS) + g]
            c0 = (t * len(DIL_PATTERNS) + g) * DIL_OUT
            y = _dot(h, w_ref[:, c0:c0 + DIL_OUT])
            if t == 0:
                y = y * ATTN_SCALE
            if dil == 1:
                o_ref[...] = y.astype(o_ref.dtype)
                continue
            for c in range(nlt):
                scr[c] = y[:, c * LANES:(c + 1) * LANES]
            for r in range(dil):
                for c in range(nlt):
                    o0 = r * DIL_OUT + c * LANES
                    o_ref[:, o0:o0 + LANES] = scr[c, pl.ds(r, bm // dil, stride=dil), :].astype(o_ref.dtype)


def _proj_dil(h, w, bm=1024):
    t, d = h.shape
    n = w.shape[1]
    out_specs, out_shapes = [], []
    for _ in range(3):
        for _, dil in DIL_PATTERNS:
            out_specs.append(pl.BlockSpec((bm // dil, dil * DIL_OUT), lambda i: (i, 0)))
            out_shapes.append(jax.ShapeDtypeStruct((t // dil, dil * DIL_OUT), BF16))
    return pl.pallas_call(
        _proj_dil_kernel,
        grid=(t // bm,),
        in_specs=[pl.BlockSpec((bm, d), lambda i: (i, 0)),
                  pl.BlockSpec((d, n), lambda i: (0, 0))],
        out_specs=out_specs,
        out_shape=out_shapes,
        scratch_shapes=[pltpu.VMEM((DIL_OUT // LANES, bm, LANES), F32)],
        compiler_params=_cparams("parallel"),
        name="proj_dil",
    )(h, w)


def _proj_dt_kernel(h_ref, w_ref, wt_ref, o_ref, ot_ref):
    h = h_ref[...]
    o_ref[...] = _dot(h, w_ref[...])
    ot_ref[...] = _dot_nt(wt_ref[...], h)


def _proj_dt(h, w, bm=512):
    t, d = h.shape
    n = w.shape[1]
    return pl.pallas_call(
        _proj_dt_kernel,
        grid=(t // bm,),
        in_specs=[pl.BlockSpec((bm, d), lambda i: (i, 0)),
                  pl.BlockSpec((d, n), lambda i: (0, 0)),
                  pl.BlockSpec((n, d), lambda i: (0, 0))],
        out_specs=[pl.BlockSpec((bm, n), lambda i: (i, 0)),
                   pl.BlockSpec((n, bm), lambda i: (0, i))],
        out_shape=[jax.ShapeDtypeStruct((t, n), F32),
                   jax.ShapeDtypeStruct((n, t), F32)],
        compiler_params=_cparams("parallel"),
        name="proj_dt",
    )(h, w, w.T)


def _proj_conv_kernel(h_ref, hp_ref, hn_ref, w_ref, cw_ref, cb_ref, xs_ref, b_out, bt_out, c_out, *ext_refs,
                      nsb):
    i = pl.program_id(0)
    bm = h_ref.shape[0]
    n = w_ref.shape[1]
    halo = SUBLANES
    pad = CONV_W // 2
    keep_prev = (i % nsb > 0).astype(F32)
    keep_next = (i % nsb < nsb - 1).astype(F32)
    nc = ext_refs[0].shape[1]
    for ext_ref, c in zip(ext_refs, range(0, n, nc)):
        w = w_ref[:, c:c + nc]
        ext_ref[halo:halo + bm, :] = _dot(h_ref[...], w)
        ext_ref[0:halo, :] = _dot(hp_ref[...], w)[BF16_ROWS - halo:, :] * keep_prev
        ext_ref[halo + bm:halo + bm + halo, :] = _dot(hn_ref[...], w)[0:halo, :] * keep_next
    rc = LANES
    for c in range(0, n, LANES):
        cs = slice(c, c + LANES)
        ext_ref = ext_refs[c // nc]
        es = slice(c % nc, c % nc + LANES)
        taps = [cw_ref[k:k + 1, cs] for k in range(CONV_W)]
        bias = cb_ref[:, cs]
        for r in range(0, bm, rc):
            n_blk = rc + 2 * halo
            blk = ext_ref[r:r + n_blk, es]
            acc = bias + blk[halo:halo + rc, :] * taps[pad]
            for k in range(CONV_W):
                if k != pad:
                    acc = acc + pltpu.roll(blk, n_blk - (halo - pad + k), 0)[0:rc, :] * taps[k]
            y = _silu(acc)
            rs = slice(r, r + rc)
            if c < D_INNER:
                xs_ref[rs, cs] = y
            elif c < D_INNER + GN:
                cc = slice(c - D_INNER, c - D_INNER + LANES)
                b_out[rs, cc] = y.astype(b_out.dtype)
                bt_out[cc, rs] = y.T.astype(bt_out.dtype)
            else:
                cc = slice(c - D_INNER - GN, c - D_INNER - GN + LANES)
                c_out[rs, cc] = y.astype(c_out.dtype)


def _proj_conv(h, w, conv_w, conv_b, batch, bm=1024):
    t, d = h.shape
    n = w.shape[1]
    s = t // batch
    nsb = s // bm
    hb = bm // BF16_ROWS
    last_hb = t // BF16_ROWS - 1
    nc = 2 * MXU_COLS
    return pl.pallas_call(
        functools.partial(_proj_conv_kernel, nsb=nsb),
        grid=(t // bm,),
        in_specs=[pl.BlockSpec((bm, d), lambda i: (i, 0)),
                  pl.BlockSpec((BF16_ROWS, d), lambda i: (jnp.maximum(i * hb - 1, 0), 0)),
                  pl.BlockSpec((BF16_ROWS, d), lambda i: (jnp.minimum((i + 1) * hb, last_hb), 0)),
                  pl.BlockSpec((d, n), lambda i: (0, 0)),
                  pl.BlockSpec((CONV_W, n), lambda i: (0, 0)),
                  pl.BlockSpec((1, n), lambda i: (0, 0))],
        out_specs=[pl.BlockSpec((bm, D_INNER), lambda i: (i, 0)),
                   pl.BlockSpec((bm, GN), lambda i: (i, 0)),
                   pl.BlockSpec((None, GN, bm), lambda i: (i // nsb, 0, i % nsb)),
                   pl.BlockSpec((bm, GN), lambda i: (i, 0))],
        out_shape=[jax.ShapeDtypeStruct((t, D_INNER), F32),
                   jax.ShapeDtypeStruct((t, GN), BF16),
                   jax.ShapeDtypeStruct((batch, GN, s), BF16),
                   jax.ShapeDtypeStruct((t, GN), BF16)],
        scratch_shapes=[pltpu.VMEM((bm + 2 * SUBLANES, nc), F32)] * (n // nc),
        compiler_params=_cparams("parallel"),
        name="proj_conv",
    )(h, h, h, w, conv_w, conv_b.reshape(1, n))


def _ssd_direction(direction, xs_ref, b_ref, bt_ref, c_ref, dt_ref, dtt_ref, bias_ref, biast_ref,
                   alog_ref, alogt_ref, dskip_ref, rep_ref, y_ref, state_ref):
    q = SSD_CHUNK
    hg = SSD_HEADS // SSD_GROUPS
    gw = hg * SSD_HEADDIM
    row = lax.broadcasted_iota(jnp.int32, (q, q), 0)
    col = lax.broadcasted_iota(jnp.int32, (q, q), 1)
    if direction == 0:
        keep = col <= row
        last = q - 1
    else:
        keep = col >= row
        last = 0
    tri = keep.astype(BF16)
    tri_t = (row <= col).astype(BF16) if direction == 0 else (row >= col).astype(BF16)

    dt = _softplus(dt_ref[...] + bias_ref[...])
    dtt = _softplus(dtt_ref[...] + biast_ref[...])
    dta = dt * (-LOG2E * jnp.exp(alog_ref[...]))
    dtat = dtt * (-LOG2E * jnp.exp(alogt_ref[...]))
    acs = _dot_exact_rhs(tri, dta)
    acst = _dot_exact_lhs(dtat, tri_t)

    def pack(x):
        wide = jnp.concatenate([x, x, x, x], axis=1)
        hi, mid, lo = _split3(wide)
        lane = lax.broadcasted_iota(jnp.int32, wide.shape, 1)
        return jnp.where(lane < x.shape[1], hi, jnp.where(lane < 2 * x.shape[1], mid, lo))

    acs_p = pack(acs)
    dt_p = pack(dt)
    lane = lax.broadcasted_iota(jnp.int32, (q, LANES), 1)
    first_half = lane < SSD_HEADDIM

    for g in range(SSD_GROUPS):
        gs = slice(g * gw, (g + 1) * gw)
        ns = slice(g * SSD_STATE, (g + 1) * SSD_STATE)
        acs_e = _dot(acs_p, rep_ref[:, gs])
        dt_e = _dot(dt_p, rep_ref[:, gs])
        xs = xs_ref[:, gs]
        xc = xs * dt_e
        xcb = xc.astype(BF16)
        last_e = acs_e[last:last + 1, :]
        xd = (xc * jnp.exp2(last_e - acs_e)).astype(BF16)
        cg = c_ref[:, ns]
        bgt = bt_ref[ns, :]
        cb = _dot(cg, bgt)
        h_in = state_ref[:, gs]
        y_off = _dot(cg, h_in.astype(BF16)) * jnp.exp2(acs_e)
        st = _dot(bgt, xd)
        state_ref[:, gs] = h_in * jnp.exp2(last_e) + st
        for jj in range(hg // 2):
            j = g * (hg // 2) + jj
            ms = []
            for hh in (2 * j, 2 * j + 1):
                ci = direction * SSD_HEADS + hh
                seg = acs[:, ci:ci + 1] - acst[ci:ci + 1, :]
                lm = jnp.exp2(jnp.where(keep, seg, NEG_BIG))
                ms.append((cb * lm).astype(BF16))
            lhs = jnp.concatenate(ms, axis=1)
            x2 = xcb[:, jj * LANES:(jj + 1) * LANES]
            zero = jnp.zeros_like(x2)
            rhs = jnp.concatenate([jnp.where(first_half, x2, zero),
                                   jnp.where(first_half, zero, x2)], axis=0)
            y = _dot(lhs, rhs) + y_off[:, jj * LANES:(jj + 1) * LANES]
            if direction == 0:
                cs = slice(j * LANES, (j + 1) * LANES)
                y = y + xs[:, jj * LANES:(jj + 1) * LANES] * dskip_ref[:, cs]
            y_ref[:, j * LANES:(j + 1) * LANES] = y


def _ssd_kernel(xs_f, b_f, bt_f, c_f, dt_f, dtt_f, xs_b, b_b, bt_b, c_b, dt_b, dtt_b,
                bias_ref, biast_ref, alog_ref, alogt_ref, dskip_ref, rep_ref,
                yf_ref, yb_ref, sf_ref, sb_ref):
    @pl.when(pl.program_id(1) == 0)
    def _():
        sf_ref[...] = jnp.zeros_like(sf_ref)
        sb_ref[...] = jnp.zeros_like(sb_ref)

    q = SSD_CHUNK
    for sub in range(SSD_CHUNKS_PER_STEP):
        rf = pl.ds(sub * q, q)
        rb = pl.ds((SSD_CHUNKS_PER_STEP - 1 - sub) * q, q)
        _ssd_direction(0, xs_f.at[rf], b_f.at[rf], bt_f.at[:, rf], c_f.at[rf], dt_f.at[rf], dtt_f.at[:, rf],
                       bias_ref, biast_ref, alog_ref, alogt_ref, dskip_ref, rep_ref.at[0], yf_ref.at[rf], sf_ref)
        _ssd_direction(1, xs_b.at[rb], b_b.at[rb], bt_b.at[:, rb], c_b.at[rb], dt_b.at[rb], dtt_b.at[:, rb],
                       bias_ref, biast_ref, alog_ref, alogt_ref, dskip_ref, rep_ref.at[1], yb_ref.at[rb], sb_ref)


def _ssd_scan(xs, bm, bmt, cm, dt, dtt, dt_bias, a_log, d_skip):
    b, s, _ = xs.shape
    q = SSD_CHUNK * SSD_CHUNKS_PER_STEP
    nc = s // q
    nh2 = 2 * SSD_HEADS

    def fwd(bi, c):
        return c

    def bwd(bi, c):
        return nc - 1 - c

    def chunk_specs(cidx):
        return [pl.BlockSpec((None, q, D_INNER), lambda bi, c: (bi, cidx(bi, c), 0)),
                pl.BlockSpec((None, q, GN), lambda bi, c: (bi, cidx(bi, c), 0)),
                pl.BlockSpec((None, GN, q), lambda bi, c: (bi, 0, cidx(bi, c))),
                pl.BlockSpec((None, q, GN), lambda bi, c: (bi, cidx(bi, c), 0)),
                pl.BlockSpec((None, q, nh2), lambda bi, c: (bi, cidx(bi, c), 0)),
                pl.BlockSpec((nh2, q), lambda bi, c: (0, bi * nc + cidx(bi, c)))]

    def const(shape):
        return pl.BlockSpec(shape, lambda bi, c: (0,) * len(shape))

    rep = np.zeros((2, 4 * nh2, D_INNER), np.float32)
    for d in range(2):
        for h in range(SSD_HEADS):
            for piece in range(3):
                rep[d, piece * nh2 + d * SSD_HEADS + h, h * SSD_HEADDIM:(h + 1) * SSD_HEADDIM] = 1.0
    rep = jnp.asarray(rep, BF16)
    dskip_e = jnp.repeat(d_skip.astype(F32), SSD_HEADDIM).reshape(1, D_INNER)
    bias = dt_bias.reshape(1, nh2).astype(F32)
    alog = a_log.reshape(1, nh2).astype(F32)

    ins = (xs, bm, bmt, cm, dt.reshape(b, s, nh2), dtt)
    return pl.pallas_call(
        _ssd_kernel,
        grid=(b, nc),
        in_specs=chunk_specs(fwd) + chunk_specs(bwd) + [
            const((1, nh2)), const((nh2, 1)), const((1, nh2)), const((nh2, 1)),
            const((1, D_INNER)), const((2, 4 * nh2, D_INNER))],
        out_specs=[pl.BlockSpec((None, q, D_INNER), lambda bi, c: (bi, c, 0)),
                   pl.BlockSpec((None, q, D_INNER), lambda bi, c: (bi, nc - 1 - c, 0))],
        out_shape=[jax.ShapeDtypeStruct((b, s, D_INNER), F32),
                   jax.ShapeDtypeStruct((b, s, D_INNER), F32)],
        scratch_shapes=[pltpu.VMEM((SSD_STATE, D_INNER), F32),
                        pltpu.VMEM((SSD_STATE, D_INNER), F32)],
        compiler_params=_cparams("parallel", "arbitrary"),
        name="ssd_scan",
    )(*ins, *ins, bias, bias.reshape(nh2, 1), alog, alog.reshape(nh2, 1), dskip_e, rep)


def _flash_kernel(shift_ref, q_ref, k_ref, v_ref, o_ref, qa_ref, ka_ref, vat_ref, acc_ref, m_ref, l_ref,
                  *, bq, bk, bk_slow):
    seq = k_ref.shape[0]
    d = HEAD_DIM
    rows = GQA_REP * bq
    shift = shift_ref[0, 0]
    first_lane = lax.broadcasted_iota(jnp.int32, (1, d), 1) == 0

    @pl.when(pl.program_id(2) == 0)
    def _():
        one_col = jnp.where(first_lane, 1.0, 0.0).astype(BF16)
        first_row = lax.broadcasted_iota(jnp.int32, (VT_ROWS - d, 1), 0) == 0
        one_row = jnp.where(first_row, 1.0, 0.0).astype(BF16)
        qa_ref[:, d:2 * d] = jnp.broadcast_to(jnp.where(first_lane, -shift, 0.0).astype(BF16), (rows, d))

        def fill(c, carry):
            off = pl.multiple_of(c * bk_slow, bk_slow)
            ka_ref[pl.ds(off, bk_slow), 0:d] = k_ref[pl.ds(off, bk_slow), :]
            ka_ref[pl.ds(off, bk_slow), d:2 * d] = jnp.broadcast_to(one_col, (bk_slow, d))
            vat_ref[0:d, pl.ds(off, bk_slow)] = v_ref[pl.ds(off, bk_slow), :].astype(F32).T.astype(BF16)
            vat_ref[d:VT_ROWS, pl.ds(off, bk_slow)] = jnp.broadcast_to(one_row, (VT_ROWS - d, bk_slow))
            return carry

        lax.fori_loop(0, seq // bk_slow, fill, 0)

    for r in range(GQA_REP):
        qa_ref[r * bq:(r + 1) * bq, 0:d] = q_ref[:, r * d:(r + 1) * d]
    use_fast = shift * 2.0 <= FAST_SOFTMAX_LOG2_RANGE

    @pl.when(use_fast)
    def _():
        qa = qa_ref[...]
        acc_t = jnp.zeros((VT_ROWS, rows), F32)
        for u in range(seq // bk):
            ks = slice(u * bk, (u + 1) * bk)
            s_t = _dot_nt(ka_ref[ks, :], qa)
            acc_t = acc_t + _dot(vat_ref[:, ks], jnp.exp2(s_t).astype(BF16))
        out = (acc_t[0:d, :] * (1.0 / acc_t[d:d + 1, :])).T
        for r in range(GQA_REP):
            o_ref[:, r * d:(r + 1) * d] = out[r * bq:(r + 1) * bq, :].astype(o_ref.dtype)

    @pl.when(jnp.logical_not(use_fast))
    def _():
        m_ref[...] = jnp.full_like(m_ref, NEG_BIG)
        l_ref[...] = jnp.zeros_like(l_ref)
        acc_ref[...] = jnp.zeros_like(acc_ref)

        def body(j, carry):
            off = pl.multiple_of(j * bk_slow, bk_slow)
            s = _dot_nt(qa_ref[:, 0:d], k_ref[pl.ds(off, bk_slow), :])
            m_prev = m_ref[...]
            m_new = jnp.maximum(m_prev, jnp.max(s, axis=-1, keepdims=True))
            alpha = jnp.exp2(m_prev - m_new)
            p = jnp.exp2(s - m_new)
            l_ref[...] = alpha * l_ref[...] + jnp.sum(p, axis=-1, keepdims=True)
            acc_ref[...] = alpha * acc_ref[...] + _dot(p.astype(BF16), v_ref[pl.ds(off, bk_slow), :])
            m_ref[...] = m_new
            return carry

        lax.fori_loop(0, seq // bk_slow, body, 0)
        inv = 1.0 / l_ref[...]
        for r in range(GQA_REP):
            rs = slice(r * bq, (r + 1) * bq)
            o_ref[:, r * d:(r + 1) * d] = (acc_ref[rs, :] * inv[rs, :]).astype(o_ref.dtype)


def _flash_gqa(q, k, v, logit_bound, bq=512, bk=1024, bk_slow=1024):
    b, s, _ = q.shape
    bk = min(bk, s)
    bk_slow = min(bk_slow, s)
    rows = GQA_REP * bq
    gw = GQA_REP * HEAD_DIM
    return pl.pallas_call(
        functools.partial(_flash_kernel, bq=bq, bk=bk, bk_slow=bk_slow),
        grid=(b, GQA_KV_HEADS, s // bq),
        in_specs=[pl.BlockSpec(memory_space=pltpu.SMEM),
                  pl.BlockSpec((None, bq, gw), lambda bi, kh, i: (bi, i, kh)),
                  pl.BlockSpec((None, s, HEAD_DIM), lambda bi, kh, i: (bi, 0, kh)),
                  pl.BlockSpec((None, s, HEAD_DIM), lambda bi, kh, i: (bi, 0, kh))],
        out_specs=pl.BlockSpec((None, bq, gw), lambda bi, kh, i: (bi, i, kh)),
        out_shape=jax.ShapeDtypeStruct((b, s, GQA_WIDTH), BF16),
        scratch_shapes=[pltpu.VMEM((rows, 2 * HEAD_DIM), BF16),
                        pltpu.VMEM((s, 2 * HEAD_DIM), BF16),
                        pltpu.VMEM((VT_ROWS, s), BF16),
                        pltpu.VMEM((rows, HEAD_DIM), F32),
                        pltpu.VMEM((rows, 1), F32),
                        pltpu.VMEM((rows, 1), F32)],
        compiler_params=_cparams("parallel", "parallel", "arbitrary"),
        name="flash_gqa",
    )(jnp.reshape(logit_bound, (1, 1)).astype(F32), q, k, v)


def _band_kernel(q_ref, kp_ref, kc_ref, kn_ref, vp_ref, vc_ref, vn_ref, bias_ref, o_ref, lse_ref,
                 *, dil, nsub, length):
    i = pl.program_id(1)
    tq = BAND_TILE
    col = lax.broadcasted_iota(jnp.int32, (tq, 2 * tq), 1)
    for u in range(nsub):
        kpos = (i * nsub + u) * tq - DIL_HALF + col
        inside = (kpos >= 0) & (kpos < length)
        rows = slice(u * tq, (u + 1) * tq)
        head = slice(u * tq - DIL_HALF, u * tq)
        tail = slice((u + 1) * tq, (u + 1) * tq + DIL_HALF)
        for r in range(dil):
            for hd in range(DIL_HEADS_PER_GROUP):
                c0 = (r * DIL_HEADS_PER_GROUP + hd) * HEAD_DIM
                sl = slice(c0, c0 + HEAD_DIM)

                def window(prev_ref, cur_ref, next_ref):
                    first = prev_ref[:, sl] if u == 0 else cur_ref[head, sl]
                    last = next_ref[:, sl] if u == nsub - 1 else cur_ref[tail, sl]
                    return jnp.concatenate([first, cur_ref[rows, sl], last], axis=0)

                s = _dot_nt(q_ref[rows, sl], window(kp_ref, kc_ref, kn_ref)) + bias_ref[hd]
                s = jnp.where(inside, s, NEG_BIG)
                m = jnp.max(s, axis=-1, keepdims=True)
                p = jnp.exp(s - m)
                l = jnp.sum(p, axis=-1, keepdims=True)
                o_ref[rows, sl] = _dot(p.astype(BF16), window(vp_ref, vc_ref, vn_ref)) / l
                lse_ref[rows, sl] = jnp.broadcast_to(m + jnp.log(l), (tq, HEAD_DIM))


def _band_attention(qr, kr, vr, bias_t, group, dil, batch):
    rows_total, width = qr.shape
    length = rows_total // batch
    nsub = max(1, BAND_TILES_PER_STEP // dil)
    bqr = nsub * BAND_TILE
    nb = length // bqr
    hb = bqr // DIL_HALF
    last_hb = length // DIL_HALF - 1
    view = lambda t: t.reshape(batch, length, width)

    cur = pl.BlockSpec((None, bqr, width), lambda bi, i: (bi, i, 0))
    prev = pl.BlockSpec((None, DIL_HALF, width), lambda bi, i: (bi, jnp.maximum(i * hb - 1, 0), 0))
    nxt = pl.BlockSpec((None, DIL_HALF, width), lambda bi, i: (bi, jnp.minimum((i + 1) * hb, last_hb), 0))
    o, lse = pl.pallas_call(
        functools.partial(_band_kernel, dil=dil, nsub=nsub, length=length),
        grid=(batch, nb),
        in_specs=[cur, prev, cur, nxt, prev, cur, nxt,
                  pl.BlockSpec((DIL_HEADS_PER_GROUP, BAND_TILE, 2 * BAND_TILE), lambda bi, i: (0, 0, 0))],
        out_specs=[cur, cur],
        out_shape=[jax.ShapeDtypeStruct((batch, length, width), F32),
                   jax.ShapeDtypeStruct((batch, length, width), F32)],
        compiler_params=_cparams("parallel", "parallel"),
        name=f"band_attn_g{group}",
    )(view(qr), view(kr), view(kr), view(kr), view(vr), view(vr), view(vr), bias_t)
    return o.reshape(rows_total, width), lse.reshape(rows_total, width)


def _t5_bucket(rel):
    nb = N_BUCKETS // 2
    max_exact = nb // 2
    ret = jnp.where(rel > 0, nb, 0)
    n = jnp.abs(rel)
    nf = jnp.maximum(n, 1).astype(F32)
    large = max_exact + (jnp.log(nf / max_exact) / math.log(REL_MAX_DIST / max_exact)
                         * (nb - max_exact)).astype(jnp.int32)
    large = jnp.minimum(large, nb - 1)
    return ret + jnp.where(n < max_exact, n, large)


def _band_bias_tiles(rel_bias):
    tq = BAND_TILE
    period = 3 * tq
    tiles = []
    for g, (_, dil) in enumerate(DIL_PATTERNS):
        dist = jnp.arange(-DIL_HALF, DIL_HALF + 1, dtype=jnp.int32) * dil
        tbl = rel_bias.astype(F32)[_t5_bucket(dist)]
        tbl = tbl[:, g * DIL_HEADS_PER_GROUP:(g + 1) * DIL_HEADS_PER_GROUP].T
        nh = tbl.shape[0]
        u = jnp.concatenate([jnp.full((nh, tq - 1), NEG_BIG, F32), tbl,
                             jnp.full((nh, period - tq - 2 * DIL_HALF), NEG_BIG, F32)], axis=1)
        rows = jnp.tile(u, (1, tq))[:, :tq * (period - 1)].reshape(nh, tq, period - 1)
        tiles.append(rows[:, :, tq - 1:3 * tq - 1])
    return tiles


def _to_token_order(src_ref, scr_ref, dil):
    if dil == 1:
        return src_ref[...]
    n = src_ref.shape[0]
    nlt = DIL_OUT // LANES
    for r in range(dil):
        for c in range(nlt):
            c0 = r * DIL_OUT + c * LANES
            scr_ref[c, pl.ds(r, n, stride=dil), :] = src_ref[:, c0:c0 + LANES]
    return jnp.concatenate([scr_ref[c] for c in range(nlt)], axis=1)


def _merge_kernel(x_ref, yf_ref, yb_ref, zs_ref, gssd_ref, ygqa_ref,
                  o0_ref, l0_ref, o1_ref, l1_ref, o2_ref, l2_ref, gates_ref,
                  wssd_ref, wgqa_ref, wdil_ref, wout_ref, gpost_ref, o_ref, *scr):
    y = (yf_ref[...] + yb_ref[...]) * zs_ref[...]
    y_ssd = _rms(y, gssd_ref[...]).astype(BF16)
    dils = [dil for _, dil in DIL_PATTERNS]
    o0, o1, o2 = (_to_token_order(r, s, dl) for r, s, dl in zip((o0_ref, o1_ref, o2_ref), scr[0:3], dils))
    l0, l1, l2 = (_to_token_order(r, s, dl) for r, s, dl in zip((l0_ref, l1_ref, l2_ref), scr[3:6], dils))
    mx = jnp.maximum(jnp.maximum(l0, l1), l2)
    e0, e1, e2 = jnp.exp(l0 - mx), jnp.exp(l1 - mx), jnp.exp(l2 - mx)
    y_dil = ((e0 * o0 + e1 * o1 + e2 * o2) / (e0 + e1 + e2)).astype(BF16)
    d = D_MODEL
    mix = gates_ref[:, 0:d] * _dot(y_ssd, wssd_ref[...])
    mix = mix + gates_ref[:, d:2 * d] * _dot(ygqa_ref[...], wgqa_ref[...])
    mix = mix + gates_ref[:, 2 * d:3 * d] * _dot(y_dil, wdil_ref[...])
    out = _dot(mix.astype(BF16), wout_ref[...])
    o_ref[...] = x_ref[...] + _rms(out, gpost_ref[...])


def _merge(x2d, yf, yb, zs, g_ssd, y_gqa, dil_parts, gates, w_ssd, w_gqa, w_dil, w_out, g_post, bm=256):
    t, d = x2d.shape
    row = lambda n: pl.BlockSpec((bm, n), lambda i: (i, 0))
    const = lambda a: pl.BlockSpec(a.shape, lambda i: (0, 0))
    (o0, l0), (o1, l1), (o2, l2) = dil_parts
    g_ssd = g_ssd.reshape(1, -1)
    g_post = g_post.reshape(1, -1)
    args = (x2d, yf, yb, zs, g_ssd, y_gqa, o0, l0, o1, l1, o2, l2, gates, w_ssd, w_gqa, w_dil, w_out, g_post)
    specs = [row(d), row(D_INNER), row(D_INNER), row(D_INNER), const(g_ssd), row(GQA_WIDTH)]
    for _, dil in DIL_PATTERNS:
        specs += [pl.BlockSpec((bm // dil, dil * DIL_OUT), lambda i: (i, 0))] * 2
    specs += [row(N_BRANCHES * d), const(w_ssd), const(w_gqa), const(w_dil), const(w_out), const(g_post)]
    return pl.pallas_call(
        _merge_kernel,
        grid=(t // bm,),
        in_specs=specs,
        out_specs=row(d),
        out_shape=jax.ShapeDtypeStruct((t, d), F32),
        scratch_shapes=[pltpu.VMEM((DIL_OUT // LANES, bm, LANES), F32)] * 6,
        compiler_params=_cparams("parallel"),
        name="merge_out",
    )(*args)


def _ffn_kernel(x_ref, pa_ref, pb_ref, gpre_ref, wg_ref, wu_ref, wd_ref, gpost_ref, gple_ref, wpg_ref, wp_ref,
                gnext_ref, *o_refs, nc, na, split_output):
    x = x_ref[...]
    h = _rms(x, gpre_ref[...]).astype(BF16)
    ff = jnp.zeros(x.shape, F32)
    for c in range(0, wg_ref.shape[1], nc):
        a = _silu(_dot(h, wg_ref[:, c:c + nc])) * _dot(h, wu_ref[:, c:c + nc])
        ff = ff + _dot(a.astype(BF16), wd_ref[c:c + nc, :])
    x = x + _rms(ff, gpost_ref[...])
    hn = _rms(x, gple_ref[...]).astype(BF16)
    gate = _sigmoid(_dot(hn, wpg_ref[...]))

    in_first = pl.program_id(0) < na
    p = jnp.where(in_first, pa_ref[...], pb_ref[...])
    out = x + _dot(p.astype(BF16), wp_ref[...]) * gate
    if split_output:
        def write_first():
            o_refs[0][...] = out

        pl.when(in_first)(write_first)
        o_refs[1][...] = out
    else:
        o_refs[0][...] = out
        o_refs[1][...] = _rms(out, gnext_ref[...]).astype(o_refs[1].dtype)


def _ffn_ple(x2d, pa, pb, layer, g_pre, w_gate, w_up, w_down, g_post, g_ple, w_ple_gate, w_ple, g_next,
             split_output, bm=512):
    t, d = x2d.shape
    ta, tb = pa.shape[1], pb.shape[1]
    na = ta // bm
    row = lambda n: pl.BlockSpec((bm, n), lambda i: (i, 0))
    first = lambda i: jnp.minimum(i, na - 1)
    second = lambda i: jnp.maximum(i - na, 0)
    const = lambda a: pl.BlockSpec(a.shape, lambda i: (0, 0), pipeline_mode=pl.Buffered(1))
    g_pre, g_post, g_ple, g_next = (g.reshape(1, d) for g in (g_pre, g_post, g_ple, g_next))
    consts = (g_pre, w_gate, w_up, w_down, g_post, g_ple, w_ple_gate, w_ple, g_next)
    specs = [row(d),
             pl.BlockSpec((None, bm, PLE_DIM), lambda i: (layer, first(i), 0)),
             pl.BlockSpec((None, bm, PLE_DIM), lambda i: (layer, second(i), 0))]
    specs += [const(a) for a in consts]
    if split_output:
        out_specs = [pl.BlockSpec((bm, d), lambda i: (first(i), 0)),
                     pl.BlockSpec((bm, d), lambda i: (second(i), 0))]
        out_shape = [jax.ShapeDtypeStruct((ta, d), F32), jax.ShapeDtypeStruct((tb, d), F32)]
    else:
        out_specs = [row(d), row(d)]
        out_shape = [jax.ShapeDtypeStruct((t, d), F32), jax.ShapeDtypeStruct((t, d), BF16)]
    nc = 256 if FFN_DIM % 256 == 0 else FFN_DIM
    return pl.pallas_call(
        functools.partial(_ffn_kernel, nc=nc, na=na, split_output=split_output),
        grid=(t // bm,),
        in_specs=specs,
        out_specs=out_specs,
        out_shape=out_shape,
        compiler_params=_cparams("arbitrary"),
        name="ffn_ple",
    )(x2d, pa, pb, *consts)


def _rope_tables(seq):
    pos = jnp.arange(seq)
    row = (pos // GRID_W).astype(F32)
    colp = (pos % GRID_W).astype(F32)
    n_pairs = HEAD_DIM // 4
    inv = ROPE_THETA ** (-jnp.arange(n_pairs, dtype=F32) / n_pairs)
    ang = jnp.concatenate([row[:, None] * inv, colp[:, None] * inv], axis=-1)
    c, s = jnp.cos(ang), jnp.sin(ang)
    return jnp.concatenate([c, c], axis=-1), jnp.concatenate([-s, s], axis=-1)


def _layer(h, x2d, b, s, pa, pb, i, P, cosf, sinf, bias_tiles, last):
    t = b * s
    offs = np.concatenate([[0], np.cumsum(IN_WIDTHS)])
    w_in = P["w_in"][i]
    wz, wxbc, wdt, wq, wk, wv, wdq, wdk, wdv, wgates = (
        w_in[:, offs[j]:offs[j + 1]] for j in range(len(IN_WIDTHS)))
    def deinterleave(w):
        lead = w.shape[:-1]
        return w.reshape(*lead, -1, HEAD_DIM // 2, 2).swapaxes(-1, -2).reshape(*lead, -1)

    wq, wk = deinterleave(wq), deinterleave(wk)
    bf = lambda w: w.astype(BF16)

    zs = _proj(h, bf(wz), BF16, act="silu", name="proj_z")
    xs, bm, bmt, cm = _proj_conv(h, bf(wxbc), P["conv_w"][i], P["conv_b"][i], b)
    dt, dtt = _proj_dt(h, bf(wdt))
    q = _proj_qk(h, bf(wq), deinterleave(P["g_q"][i]), cosf, sinf, ATTN_SCALE * LOG2E, s, name="proj_q")
    k = _proj_qk(h, bf(wk), deinterleave(P["g_k"][i]), cosf, sinf, 1.0, s, name="proj_k")
    v = _proj(h, bf(wv), BF16, name="proj_v")
    dil_qkv = _proj_dil(h, bf(jnp.concatenate([wdq, wdk, wdv], axis=1)))
    gates = _proj(h, bf(wgates), BF16, act="sigmoid", name="proj_gates")

    yf, yb = _ssd_scan(xs.reshape(b, s, D_INNER), bm.reshape(b, s, GN), bmt, cm.reshape(b, s, GN), dt, dtt,
                       P["dt_bias"][i], P["a_log"][i], P["d_skip"][i])

    logit_bound = (HEAD_DIM * ATTN_SCALE * LOG2E * NORM_SLACK
                   * jnp.max(jnp.abs(P["g_q"][i])) * jnp.max(jnp.abs(P["g_k"][i])))
    y_gqa = _flash_gqa(q.reshape(b, s, GQA_WIDTH), k.reshape(b, s, GQA_KV_WIDTH),
                       v.reshape(b, s, GQA_KV_WIDTH), logit_bound)

    dil_parts = []
    ng = len(DIL_PATTERNS)
    for g, (_, dil) in enumerate(DIL_PATTERNS):
        dil_parts.append(_band_attention(dil_qkv[g], dil_qkv[ng + g], dil_qkv[2 * ng + g],
                                         bias_tiles[g], g, dil, b))

    x2d = _merge(x2d, yf.reshape(t, D_INNER), yb.reshape(t, D_INNER), zs, P["g_ssd"][i],
                 y_gqa.reshape(t, GQA_WIDTH), dil_parts, gates,
                 bf(P["w_br_ssd"][i]), bf(P["w_br_gqa"][i]), bf(P["w_br_dil"][i]), bf(P["w_out"][i]),
                 P["g_post_mix"][i])
    g_next = P["g_pre_mix"][i if last else i + 1]
    return _ffn_ple(x2d, pa, pb, i, P["g_pre_ffn"][i], bf(P["w_gate"][i]), bf(P["w_up"][i]),
                    bf(P["w_down"][i]), P["g_post_ffn"][i], P["g_ple"][i], bf(P["w_ple_gate"][i]),
                    bf(P["w_ple"][i]), g_next, split_output=last)


def kernel(x_prompt, x_sample, p_prompt, p_sample, w_in, conv_w, conv_b, dt_bias, a_log, d_skip, g_ssd, g_q, g_k, w_br_ssd, w_br_gqa, w_br_dil, w_out, g_pre_mix, g_post_mix, g_pre_ffn, g_post_ffn, w_gate, w_up, w_down, w_ple, g_ple, w_ple_gate, rel_bias):
    P = dict(w_in=w_in, conv_w=conv_w, conv_b=conv_b, dt_bias=dt_bias, a_log=a_log, d_skip=d_skip,
             g_ssd=g_ssd, g_q=g_q, g_k=g_k, w_br_ssd=w_br_ssd, w_br_gqa=w_br_gqa, w_br_dil=w_br_dil,
             w_out=w_out, g_pre_mix=g_pre_mix, g_post_mix=g_post_mix, g_pre_ffn=g_pre_ffn,
             g_post_ffn=g_post_ffn, w_gate=w_gate, w_up=w_up, w_down=w_down, w_ple=w_ple,
             g_ple=g_ple, w_ple_gate=w_ple_gate)
    ba, s, d = x_prompt.shape
    bb = x_sample.shape[0]
    assert x_sample.shape[1:] == (s, d), "both request groups must have the same sequence length"
    depth = p_prompt.shape[0]
    pa = p_prompt.reshape(depth, ba * s, PLE_DIM)
    pb = p_sample.reshape(depth, bb * s, PLE_DIM)
    cosf, sinf = _rope_tables(s)
    bias_tiles = _band_bias_tiles(rel_bias)
    h, x2d = _prenorm_cat(x_prompt.reshape(ba * s, d), x_sample.reshape(bb * s, d), g_pre_mix[0])
    for i in range(depth - 1):
        x2d, h = _layer(h, x2d, ba + bb, s, pa, pb, i, P, cosf, sinf, bias_tiles, last=False)
    ya, yb = _layer(h, x2d, ba + bb, s, pa, pb, depth - 1, P, cosf, sinf, bias_tiles, last=True)
    return (ya.reshape(ba, s, d), yb.reshape(bb, s, d))
```
